```python
import jax, jax.numpy as jnp
from jax import lax
import numpy as np

D_MODEL = 1024
BATCH = 32
SEQ = 2048
DEPTH = 1
DEC_BATCH = 4
DEC_SEQ = 4096
PAST_LEN = 128

CONV_W = D_MODEL // 2
POOL_W = D_MODEL // 2
N_POOL_GROUPS = 4
POOL_GC = POOL_W // N_POOL_GROUPS
POOL_WINDOWS = (2, 4, 8, 16)
KERNEL_SIZE = 31
IN_COLS = 2 * CONV_W + POOL_W
N_EXPERTS = 16
CAPACITY_FACTOR = 2
D_EXPERT = 2 * D_MODEL
PLE_DIM = 256
EPS = 1e-6

kernel_name = "hybrid_conv_pool_ec_encoder"


def rmsnorm(x, g):
    xf = x.astype(jnp.float32)
    y = xf * lax.rsqrt(jnp.mean(xf * xf, axis=-1, keepdims=True) + EPS)
    return (y * g.astype(jnp.float32)).astype(x.dtype)


def layernorm(x, g, b):
    xf = x.astype(jnp.float32)
    mu = jnp.mean(xf, axis=-1, keepdims=True)
    var = jnp.mean(jnp.square(xf - mu), axis=-1, keepdims=True)
    y = (xf - mu) * lax.rsqrt(var + EPS)
    return (y * g.astype(jnp.float32) + b.astype(jnp.float32)).astype(x.dtype)


def conv_module(za, zg, conv_k, conv_b, ln_g, ln_b):
    u = za * jax.nn.sigmoid(zg)
    pad = KERNEL_SIZE // 2
    u = lax.conv_general_dilated(u, conv_k.astype(u.dtype), window_strides=(1,),
                                 padding=[(pad, pad)],
                                 dimension_numbers=("NWC", "WIO", "NWC"),
                                 feature_group_count=CONV_W)
    u = u + conv_b
    u = layernorm(u, ln_g, ln_b)
    return jax.nn.silu(u)


def centred_mean_minus_self(u, w):
    L = u.shape[1]
    uf = u.astype(jnp.float32)
    c = jnp.cumsum(uf, axis=1)
    c = jnp.concatenate([jnp.zeros_like(c[:, :1]), c], axis=1)
    t = jnp.arange(L)
    lo = jnp.maximum(t - w // 2, 0)
    hi = jnp.minimum(t + w // 2 - 1, L - 1)
    s = jnp.take(c, hi + 1, axis=1) - jnp.take(c, lo, axis=1)
    cnt = (hi - lo + 1).astype(jnp.float32)[None, :, None]
    return (s / cnt - uf).astype(u.dtype)


def pool_module(zp, pool_w, pool_scale):
    B, L, _ = zp.shape
    zg = zp.reshape(B, L, N_POOL_GROUPS, POOL_GC)
    pooled = jnp.stack([centred_mean_minus_self(zg[:, :, g], POOL_WINDOWS[g])
                        for g in range(N_POOL_GROUPS)], axis=2)
    y = jnp.einsum("blgc,gcd->blgd", pooled, pool_w) * pool_scale
    return y.reshape(B, L, POOL_W)


def expert_choice_moe(h, w_router, w_gate, w_up, w_down):
    B, L, D = h.shape
    n_tok = B * L
    t = h.reshape(n_tok, D)
    probs = jax.nn.softmax((t @ w_router).astype(jnp.float32), axis=-1)
    cap = CAPACITY_FACTOR * n_tok // N_EXPERTS
    gates, idx = lax.top_k(probs.T, cap)
    xe = jnp.take(t, idx, axis=0)
    a = jnp.einsum("ecd,edf->ecf", xe, w_gate)
    b = jnp.einsum("ecd,edf->ecf", xe, w_up)
    ye = jnp.einsum("ecf,efd->ecd", jax.nn.silu(a) * b, w_down)
    ye = ye * gates[..., None].astype(ye.dtype)
    out = jnp.zeros_like(t).at[idx.reshape(-1)].add(ye.reshape(-1, D))
    return out.reshape(B, L, D)


def run_trunk(x, p, norm1_g, w_in, conv_k, conv_b, conv_ln_g, conv_ln_b, pool_w, pool_scale,
              w_out, norm2_g, w_router, w_gate, w_up, w_down, w_ple, w_pg, b_pg, final_g):
    for i in range(DEPTH):
        h = rmsnorm(x, norm1_g[i])
        z = h @ w_in[i]
        za = z[..., :CONV_W]
        zg = z[..., CONV_W:2 * CONV_W]
        zp = z[..., 2 * CONV_W:]
        yc = conv_module(za, zg, conv_k[i], conv_b[i], conv_ln_g[i], conv_ln_b[i])
        yp = pool_module(zp, pool_w[i], pool_scale[i])
        x = x + jnp.concatenate([yc, yp], axis=-1) @ w_out[i]
        x = x + expert_choice_moe(rmsnorm(x, norm2_g[i]), w_router[i], w_gate[i], w_up[i], w_down[i])
        gate = jax.nn.sigmoid(x @ w_pg[i] + b_pg[i])
        x = x + gate * (p[i] @ w_ple[i])
    return rmsnorm(x, final_g)


def setup_inputs(seed: int = 0) -> dict:
    key = jax.random.key(seed)
    ks = jax.random.split(key, 24)
    f = jnp.float32
    nrm = lambda k, shape, scale: jax.random.normal(k, shape, f) * scale
    return {
        "x_prompt": nrm(ks[0], (BATCH, SEQ, D_MODEL), 1.0),
        "x_sample": nrm(ks[1], (DEC_BATCH, DEC_SEQ, D_MODEL), 1.0),
        "p_prompt": nrm(ks[2], (DEPTH, BATCH, SEQ, PLE_DIM), 1.0),
        "p_sample": nrm(ks[3], (DEPTH, DEC_BATCH, DEC_SEQ, PLE_DIM), 1.0),
        "norm1_g": 1.0 + nrm(ks[4], (DEPTH, D_MODEL), 0.02),
        "w_in": nrm(ks[5], (DEPTH, D_MODEL, IN_COLS), D_MODEL ** -0.5),
        "conv_k": nrm(ks[6], (DEPTH, KERNEL_SIZE, 1, CONV_W), KERNEL_SIZE ** -0.5),
        "conv_b": nrm(ks[7], (DEPTH, CONV_W), 0.02),
        "conv_ln_g": 1.0 + nrm(ks[8], (DEPTH, CONV_W), 0.02),
        "conv_ln_b": nrm(ks[9], (DEPTH, CONV_W), 0.02),
        "pool_w": nrm(ks[10], (DEPTH, N_POOL_GROUPS, POOL_GC, POOL_GC), POOL_GC ** -0.5),
        "pool_scale": 1.0 + nrm(ks[11], (DEPTH, N_POOL_GROUPS, POOL_GC), 0.1),
        "w_out": nrm(ks[12], (DEPTH, D_MODEL, D_MODEL), D_MODEL ** -0.5),
        "norm2_g": 1.0 + nrm(ks[13], (DEPTH, D_MODEL), 0.02),
        "w_router": nrm(ks[14], (DEPTH, D_MODEL, N_EXPERTS), D_MODEL ** -0.5),
        "w_gate": nrm(ks[15], (DEPTH, N_EXPERTS, D_MODEL, D_EXPERT), D_MODEL ** -0.5),
        "w_up": nrm(ks[16], (DEPTH, N_EXPERTS, D_MODEL, D_EXPERT), D_MODEL ** -0.5),
        "w_down": nrm(ks[17], (DEPTH, N_EXPERTS, D_EXPERT, D_MODEL), D_EXPERT ** -0.5),
        "w_ple": nrm(ks[18], (DEPTH, PLE_DIM, D_MODEL), PLE_DIM ** -0.5),
        "w_pg": nrm(ks[19], (DEPTH, D_MODEL, D_MODEL), D_MODEL ** -0.5),
        "b_pg": nrm(ks[20], (DEPTH, D_MODEL), 0.02),
        "final_g": 1.0 + nrm(ks[21], (D_MODEL,), 0.02),
    }


def reference(x_prompt, x_sample, p_prompt, p_sample, norm1_g, w_in, conv_k, conv_b, conv_ln_g,
              conv_ln_b, pool_w, pool_scale, w_out, norm2_g, w_router, w_gate, w_up, w_down,
              w_ple, w_pg, b_pg, final_g):
    y_prompt = run_trunk(x_prompt, p_prompt, norm1_g, w_in, conv_k, conv_b, conv_ln_g, conv_ln_b,
                         pool_w, pool_scale, w_out, norm2_g, w_router, w_gate, w_up, w_down,
                         w_ple, w_pg, b_pg, final_g)
    y_sample = run_trunk(x_sample, p_sample, norm1_g, w_in, conv_k, conv_b, conv_ln_g, conv_ln_b,
                         pool_w, pool_scale, w_out, norm2_g, w_router, w_gate, w_up, w_down,
                         w_ple, w_pg, b_pg, final_g)
    return (y_prompt, y_sample)
```

```python
import functools

import jax
import jax.numpy as jnp
from jax import lax
from jax.experimental import pallas as pl
from jax.experimental.pallas import tpu as pltpu

D_MODEL = 1024
CONV_W = 512
POOL_W = 512
POOL_GROUPS = 4
POOL_GC = 128
POOL_WINDOWS = (2, 4, 8, 16)
KERNEL_SIZE = 31
IN_COLS = 2 * CONV_W + POOL_W
N_EXPERTS = 16
CAPACITY_FACTOR = 2
D_EXPERT = 2 * D_MODEL
PLE_DIM = 256
EPS = 1e-6

LANES = 128
SUBLANES = 8
HALO = 16
VMEM_LIMIT = 56 * 1024 * 1024
ROW_W = D_MODEL + LANES

SEQ_TILE = 512
SLOT_CHUNK = 256
COMBINE_TILE = 512
SLOT_WINDOW = 128

BF16 = jnp.bfloat16
F32 = jnp.float32
I32 = jnp.int32


def _dot(a, b):
    return jnp.dot(a, b, preferred_element_type=F32)


def _rms(x, g):
    return x * lax.rsqrt(jnp.mean(x * x, axis=-1, keepdims=True) + EPS) * g


def _front_kernel(xp_ref, xm_ref, xn_ref, g1_ref, win_ref, ck_ref, cb_ref, lg_ref, lb_ref,
                  pw_ref, ps_ref, wout_ref, g2_ref, wrt_ref, wr_ref,
                  x1_ref, row_ref, pr_ref, u_ref, zp_ref, *, seq_len, tile):
    t = pl.program_id(1)
    nt = pl.num_programs(1)
    rows = tile + 2 * HALO

    xs = jnp.concatenate([xp_ref[...], xm_ref[...], xn_ref[...]], axis=0)
    h = _rms(xs, g1_ref[...]).astype(BF16)
    z = _dot(h, win_ref[...])

    ridx = lax.broadcasted_iota(I32, (rows, LANES), 0)
    valid = jnp.logical_and(jnp.logical_or(ridx >= HALO, t > 0),
                            jnp.logical_or(ridx < HALO + tile, t < nt - 1))
    validf = jnp.where(valid, 1.0, 0.0).astype(F32)
    validc = jnp.concatenate([validf] * (CONV_W // LANES), axis=1)

    za = z[:, :CONV_W]
    zg = z[:, CONV_W:2 * CONV_W]
    u_ref[0] = za * jax.nn.sigmoid(zg) * validc
    zp_ref[...] = z[:, 2 * CONV_W:] * validc
    for r in range(1, SUBLANES):
        u_ref[r, pl.ds(0, rows - SUBLANES), :] = u_ref[0, pl.ds(r, rows - SUBLANES), :]

    conv = None
    for k in range(KERNEL_SIZE):
        off = HALO - KERNEL_SIZE // 2 + k
        r = off % SUBLANES
        term = u_ref[r, pl.ds(off - r, tile), :] * ck_ref[k:k + 1, :]
        conv = term if conv is None else conv + term
    conv = conv + cb_ref[...]
    mu = jnp.mean(conv, axis=-1, keepdims=True)
    cen = conv - mu
    var = jnp.mean(cen * cen, axis=-1, keepdims=True)
    yc = cen * lax.rsqrt(var + EPS) * lg_ref[...] + lb_ref[...]
    yc = yc * jax.nn.sigmoid(yc)

    pos = lax.broadcasted_iota(I32, (tile, LANES), 0) + t * tile
    yps = []
    for g, w in enumerate(POOL_WINDOWS):
        cols = pl.ds(g * POOL_GC, POOL_GC)
        s = None
        for j in range(-(w // 2), w // 2):
            term = zp_ref[pl.ds(HALO + j, tile), cols]
            s = term if s is None else s + term
        lo = jnp.maximum(pos - w // 2, 0)
        hi = jnp.minimum(pos + w // 2 - 1, seq_len - 1)
        cnt = (hi - lo + 1).astype(F32)
        pooled = s / cnt - zp_ref[pl.ds(HALO, tile), cols]
        yp = _dot(pooled.astype(BF16), pw_ref[g]) * ps_ref[:, g * POOL_GC:(g + 1) * POOL_GC]
        yps.append(yp)

    y = jnp.concatenate([yc] + yps, axis=1).astype(BF16)
    x1 = xm_ref[...] + _dot(y, wout_ref[...])
    x1_ref[...] = x1

    h2f = _rms(x1, g2_ref[...])
    h2 = h2f.astype(BF16)
    logits = lax.dot_general(wrt_ref[...], h2, (((1,), (1,)), ((), ())),
                             preferred_element_type=F32)
    m = jnp.max(logits, axis=0, keepdims=True)
    ex = jnp.exp(logits - m)
    pr_ref[...] = ex / jnp.sum(ex, axis=0, keepdims=True)
    lt = _dot(h2, wr_ref[...])
    live = lax.broadcasted_iota(I32, (tile, LANES), 1) < N_EXPERTS
    lt = jnp.where(live, lt, -jnp.inf)
    et = jnp.exp(lt - jnp.max(lt, axis=1, keepdims=True))
    row_ref[...] = jnp.concatenate([h2f, et / jnp.sum(et, axis=1, keepdims=True)], axis=1)


def _front(x, g1, w_in, conv_k, conv_b, ln_g, ln_b, pool_w, pool_scale, w_out, g2, w_rt, w_r):
    b, l, d = x.shape
    tile = min(SEQ_TILE, l)
    nt = l // tile
    hpt = tile // HALO
    nh = l // HALO
    n = b * l
    const = lambda shape: pl.BlockSpec(shape, lambda i, j: (0,) * len(shape))
    kern = functools.partial(_front_kernel, seq_len=l, tile=tile)
    return pl.pallas_call(
        kern,
        grid=(b, nt),
        in_specs=[
            pl.BlockSpec((None, HALO, d), lambda i, j: (i, jnp.maximum(j * hpt - 1, 0), 0)),
            pl.BlockSpec((None, tile, d), lambda i, j: (i, j, 0)),
            pl.BlockSpec((None, HALO, d), lambda i, j: (i, jnp.minimum((j + 1) * hpt, nh - 1), 0)),
            const((1, d)), const((d, IN_COLS)), const((KERNEL_SIZE, CONV_W)), const((1, CONV_W)),
            const((1, CONV_W)), const((1, CONV_W)), const((POOL_GROUPS, POOL_GC, POOL_GC)),
            const((1, POOL_W)), const((d, d)), const((1, d)), const((N_EXPERTS, d)), const((d, LANES)),
        ],
        out_specs=[
            pl.BlockSpec((None, tile, d), lambda i, j: (i, j, 0)),
            pl.BlockSpec((tile, ROW_W), lambda i, j: (i * nt + j, 0)),
            pl.BlockSpec((N_EXPERTS, tile), lambda i, j: (0, i * nt + j)),
        ],
        out_shape=[
            jax.ShapeDtypeStruct((b, l, d), F32),
            jax.ShapeDtypeStruct((n, ROW_W), F32),
            jax.ShapeDtypeStruct((N_EXPERTS, n), F32),
        ],
        scratch_shapes=[pltpu.VMEM((SUBLANES, tile + 2 * HALO, CONV_W), F32),
                        pltpu.VMEM((tile + 2 * HALO, POOL_W), F32)],
        compiler_params=pltpu.CompilerParams(
            dimension_semantics=("arbitrary", "arbitrary"), vmem_limit_bytes=VMEM_LIMIT),
        name="front",
    )(x, x, x, g1, w_in, conv_k, conv_b, ln_g, ln_b, pool_w, pool_scale, w_out, g2, w_rt, w_r)


def _select_kernel(p_ref, pos_ref, rank_ref, off_ref, lo_ref, hi_ref, tot_ref, *, cap, chunk):
    e, r, _ = p_ref.shape
    p = p_ref[...]
    capf = jnp.float32(cap)

    def count_ge(thr):
        hit = jnp.where(p >= thr, 1.0, 0.0).astype(F32)
        return jnp.sum(jnp.sum(hit, axis=1, keepdims=True), axis=2, keepdims=True)

    def bisect(i, bits):
        cand = jnp.bitwise_or(bits, jnp.left_shift(jnp.int32(1), 30 - i))
        return jnp.where(count_ge(lax.bitcast_convert_type(cand, F32)) >= capf, cand, bits)

    tau_bits = lax.fori_loop(0, 31, bisect, jnp.zeros((e, 1, 1), I32))
    tau = lax.bitcast_convert_type(tau_bits, F32)
    above_tau = p >= lax.bitcast_convert_type(tau_bits + 1, F32)
    at_tau = jnp.logical_and(p >= tau, jnp.logical_not(above_tau))

    gt = jnp.where(above_tau, 1.0, 0.0).astype(BF16).reshape(e * r, LANES)
    eq = jnp.where(at_tau, 1.0, 0.0).astype(BF16).reshape(e * r, LANES)
    ri = lax.broadcasted_iota(I32, (LANES, LANES), 0)
    ci = lax.broadcasted_iota(I32, (LANES, LANES), 1)
    upper = jnp.where(ri <= ci, 1.0, 0.0).astype(BF16)
    ones = jnp.ones((LANES, LANES), BF16)
    rr = lax.broadcasted_iota(I32, (r, r), 0)
    rc = lax.broadcasted_iota(I32, (r, r), 1)
    lower = jnp.where(rc < rr, 1.0, 0.0).astype(BF16)

    def prefix(mask):
        within = _dot(mask, upper).reshape(e, r, LANES)
        tot_ref[...] = _dot(mask, ones).reshape(e, r, LANES)
        offs = [_dot(lower, tot_ref[i].astype(BF16)) for i in range(e)]
        return within, jnp.stack(offs, axis=0), tot_ref[...]

    gt_in, gt_off, gt_tot = prefix(gt)
    eq_in, eq_off, eq_tot = prefix(eq)
    n_gt = gt_off[:, r - 1:r, :] + gt_tot[:, r - 1:r, :]
    need = capf - n_gt
    eq_incl = eq_in + eq_off
    sel = jnp.logical_or(above_tau, jnp.logical_and(at_tau, eq_incl <= need))
    rank = gt_in + gt_off + jnp.minimum(eq_incl, need)
    rank_ref[...] = rank.astype(I32)
    pos_ref[...] = jnp.where(sel, rank - 1.0, -1.0).astype(I32)
    off_ref[...] = (gt_off + jnp.minimum(eq_off, need)).astype(I32)
    row_end = gt_off + gt_tot + jnp.minimum(eq_off + eq_tot, need)
    first = (lax.broadcasted_iota(I32, (1, 1, LANES), 2) * chunk).astype(F32)
    lo_ref[...] = jnp.sum(jnp.where(row_end <= first, 1.0, 0.0), axis=1, keepdims=True).astype(I32)
    hi_ref[...] = jnp.sum(jnp.where(row_end <= first + (chunk - 1.0), 1.0, 0.0),
                          axis=1, keepdims=True).astype(I32)


def _select(probs3, cap, chunk):
    e, r, _ = probs3.shape
    full = lambda s: pl.BlockSpec(s, lambda: (0,) * len(s))
    big = jax.ShapeDtypeStruct((e, r, LANES), I32)
    small = jax.ShapeDtypeStruct((e, 1, LANES), I32)
    return pl.pallas_call(
        functools.partial(_select_kernel, cap=cap, chunk=chunk),
        in_specs=[full((e, r, LANES))],
        out_specs=[full((e, r, LANES))] * 3 + [full((e, 1, LANES))] * 2,
        out_shape=[big, big, big, small, small],
        scratch_shapes=[pltpu.VMEM((e, r, LANES), F32)],
        compiler_params=pltpu.CompilerParams(vmem_limit_bytes=VMEM_LIMIT),
        name="select",
    )(probs3)


def _invert_kernel(lo_ref, hi_ref, rank_ref, idx_ref, *, n_chunks, chunk):
    e = pl.program_id(0)

    def one_chunk(c, carry):
        lo = lo_ref[e * n_chunks + c]
        hi = hi_ref[e * n_chunks + c]
        slot = lax.broadcasted_iota(I32, (chunk, LANES), 0) + c * chunk

        def one_row(r, acc):
            return acc + jnp.where(rank_ref[pl.ds(r, 1), :] <= slot, 1.0, 0.0)

        acc = lax.fori_loop(lo, hi + 1, one_row, jnp.zeros((chunk, LANES), F32))
        cnt = jnp.sum(acc.T, axis=0, keepdims=True)
        idx_ref[pl.ds(c, 1), :] = cnt.astype(I32) + lo * LANES
        return carry

    lax.fori_loop(0, n_chunks, one_chunk, 0)


def _invert(rank3, lo_row, hi_row, n_chunks, chunk):
    e, r, _ = rank3.shape
    grid_spec = pltpu.PrefetchScalarGridSpec(
        num_scalar_prefetch=2,
        grid=(e,),
        in_specs=[pl.BlockSpec((None, r, LANES), lambda i, *_: (i, 0, 0))],
        out_specs=pl.BlockSpec((None, n_chunks, chunk), lambda i, *_: (i, 0, 0)),
    )
    return pl.pallas_call(
        functools.partial(_invert_kernel, n_chunks=n_chunks, chunk=chunk),
        grid_spec=grid_spec,
        out_shape=jax.ShapeDtypeStruct((e, n_chunks, chunk), I32),
        compiler_params=pltpu.CompilerParams(dimension_semantics=("arbitrary",)),
        name="invert",
    )(lo_row, hi_row, rank3)


def _experts_kernel(cur_ref, nxt_ref, row_hbm, wg_ref, wu_ref, wd_ref, ye_ref, buf, sem,
                    *, n_chunks, chunk):
    e = pl.program_id(0)
    step = e * n_chunks + pl.program_id(1)
    n_steps = N_EXPERTS * n_chunks
    slot = lax.rem(step, 2)

    def gather(idx_ref, s):
        def one(r, carry):
            pltpu.make_async_copy(row_hbm.at[pl.ds(idx_ref[0, r], 1), :],
                                  buf.at[s, pl.ds(r, 1), :], sem.at[s]).start()
            return carry
        lax.fori_loop(0, chunk, one, 0, unroll=8)

    @pl.when(step == 0)
    def _():
        gather(cur_ref, 0)

    @pl.when(step + 1 < n_steps)
    def _():
        gather(nxt_ref, 1 - slot)

    pltpu.make_async_copy(row_hbm.at[pl.ds(0, chunk), :], buf.at[slot], sem.at[slot]).wait()

    xe = buf[slot, :, :D_MODEL].astype(BF16)
    lane = lax.broadcasted_iota(I32, (chunk, LANES), 1)
    gate = jnp.sum(jnp.where(lane == e, buf[slot, :, D_MODEL:], 0.0), axis=1, keepdims=True)
    a = _dot(xe, wg_ref[...])
    b = _dot(xe, wu_ref[...])
    hid = (a * jax.nn.sigmoid(a) * b).astype(BF16)
    ye_ref[...] = (_dot(hid, wd_ref[...]) * gate).astype(BF16)


def _experts(rows, idx, wg, wu, wd, cap):
    d = D_MODEL
    e, n_chunks, chunk = idx.shape
    n_steps = e * n_chunks
    idx = idx.reshape(n_steps, 1, chunk)
    smem = lambda f: pl.BlockSpec((None, 1, chunk), f, memory_space=pltpu.SMEM)
    return pl.pallas_call(
        functools.partial(_experts_kernel, n_chunks=n_chunks, chunk=chunk),
        grid=(e, n_chunks),
        in_specs=[
            smem(lambda i, c: (i * n_chunks + c, 0, 0)),
            smem(lambda i, c: (jnp.minimum(i * n_chunks + c + 1, n_steps - 1), 0, 0)),
            pl.BlockSpec(memory_space=pl.ANY),
            pl.BlockSpec((None, d, D_EXPERT), lambda i, c: (i, 0, 0)),
            pl.BlockSpec((None, d, D_EXPERT), lambda i, c: (i, 0, 0)),
            pl.BlockSpec((None, D_EXPERT, d), lambda i, c: (i, 0, 0)),
        ],
        out_specs=pl.BlockSpec((None, chunk, d), lambda i, c: (i, c, 0)),
        out_shape=jax.ShapeDtypeStruct((e, cap, d), BF16),
        scratch_shapes=[pltpu.VMEM((2, chunk, ROW_W), F32), pltpu.SemaphoreType.DMA((2,))],
        compiler_params=pltpu.CompilerParams(
            dimension_semantics=("arbitrary", "arbitrary"), vmem_limit_bytes=VMEM_LIMIT),
        name="experts",
    )(idx, idx, rows, wg, wu, wd)


def _combine_kernel(ws_ref, wn_ref, x1_ref, pos_ref, ye_hbm, p_ref, wple_ref, wpg_ref, bpg_ref,
                    fg_ref, y_ref, ybuf, sem, xbuf, xsem, acc_ref, *, cap, tile, sw):
    i = pl.program_id(0)
    n = pl.num_programs(0)
    slot = lax.rem(i, 2)

    def copies(step, s):
        return [pltpu.make_async_copy(
            ye_hbm.at[pl.ds(pl.multiple_of(e * cap + ws_ref[step * N_EXPERTS + e], 16), sw), :],
            ybuf.at[s, pl.ds(e * sw, sw), :], sem.at[s]) for e in range(N_EXPERTS)]

    @pl.when(i == 0)
    def _():
        for cp in copies(0, 0):
            cp.start()

    @pl.when(i + 1 < n)
    def _():
        for cp in copies(i + 1, 1 - slot):
            cp.start()

    for cp in copies(i, slot):
        cp.wait()

    pos = pos_ref[...]
    lane = lax.broadcasted_iota(I32, (tile, sw), 1)
    hits = []
    for e in range(N_EXPERTS):
        rel = pos[:, e:e + 1] - ws_ref[i * N_EXPERTS + e]
        hits.append(jnp.where(rel == lane, 1.0, 0.0).astype(BF16))
    acc_ref[...] = _dot(jnp.concatenate(hits, axis=1), ybuf[slot])

    for e in range(N_EXPERTS):
        first = ws_ref[i * N_EXPERTS + e]

        def extra(w, carry, e=e, first=first):
            want = first + w * sw
            start = pl.multiple_of(jnp.minimum(want, cap - sw), 16)
            cp = pltpu.make_async_copy(ye_hbm.at[pl.ds(e * cap + start, sw), :], xbuf, xsem)
            cp.start()
            cp.wait()
            col = pos[:, e:e + 1]
            hit = jnp.logical_and(col - start == lane, col >= want)
            acc_ref[...] += _dot(jnp.where(hit, 1.0, 0.0).astype(BF16), xbuf[...])
            return carry

        lax.fori_loop(1, wn_ref[i * N_EXPERTS + e], extra, 0)

    x2 = x1_ref[...] + acc_ref[...]
    gate = jax.nn.sigmoid(_dot(x2.astype(BF16), wpg_ref[...]) + bpg_ref[...])
    x3 = x2 + gate * _dot(p_ref[...].astype(BF16), wple_ref[...])
    y_ref[...] = _rms(x3, fg_ref[...])


def _combine(x1, pos_tok, ye, p, w_ple, w_pg, b_pg, fg, win_start, win_n, cap):
    n, d = x1.shape
    tile = min(COMBINE_TILE, n)
    sw = min(SLOT_WINDOW, cap)
    const = lambda shape: pl.BlockSpec(shape, lambda i, *_: (0,) * len(shape))
    grid_spec = pltpu.PrefetchScalarGridSpec(
        num_scalar_prefetch=2,
        grid=(n // tile,),
        in_specs=[
            pl.BlockSpec((tile, d), lambda i, *_: (i, 0)),
            pl.BlockSpec((tile, N_EXPERTS), lambda i, *_: (i, 0)),
            pl.BlockSpec(memory_space=pl.ANY),
            pl.BlockSpec((tile, PLE_DIM), lambda i, *_: (i, 0)),
            const((PLE_DIM, d)), const((d, d)), const((1, d)), const((1, d)),
        ],
        out_specs=pl.BlockSpec((tile, d), lambda i, *_: (i, 0)),
        scratch_shapes=[pltpu.VMEM((2, N_EXPERTS * sw, d), BF16), pltpu.SemaphoreType.DMA((2,)),
                        pltpu.VMEM((sw, d), BF16), pltpu.SemaphoreType.DMA(()),
                        pltpu.VMEM((tile, d), F32)],
    )
    return pl.pallas_call(
        functools.partial(_combine_kernel, cap=cap, tile=tile, sw=sw),
        grid_spec=grid_spec,
        out_shape=jax.ShapeDtypeStruct((n, d), F32),
        compiler_params=pltpu.CompilerParams(
            dimension_semantics=("arbitrary",), vmem_limit_bytes=VMEM_LIMIT),
        name="combine",
    )(win_start, win_n, x1, pos_tok, ye.reshape(N_EXPERTS * cap, d), p, w_ple, w_pg, b_pg, fg)


def _trunk(x, p, w):
    b, l, d = x.shape
    n = b * l
    cap = CAPACITY_FACTOR * n // N_EXPERTS
    rows = n // LANES

    x1, tok_rows, probs = _front(x, w["g1"], w["w_in"], w["conv_k"], w["conv_b"], w["ln_g"],
                                 w["ln_b"], w["pool_w"], w["pool_scale"], w["w_out"], w["g2"],
                                 w["w_rt"], w["w_r"])

    chunk = min(SLOT_CHUNK, cap)
    n_chunks = cap // chunk
    pos3, rank3, off3, lo3, hi3 = _select(probs.reshape(N_EXPERTS, rows, LANES), cap, chunk)

    lo_row = lo3[:, 0, :n_chunks].reshape(-1)
    hi_row = hi3[:, 0, :n_chunks].reshape(-1)
    idx = _invert(rank3, lo_row, hi_row, n_chunks, chunk)
    ye = _experts(tok_rows, idx, w["w_gate"], w["w_up"], w["w_down"], cap)

    tile = min(COMBINE_TILE, n)
    sw = min(SLOT_WINDOW, cap)
    starts = off3[:, ::tile // LANES, 0]
    ends = jnp.concatenate([starts[:, 1:], jnp.full((N_EXPERTS, 1), cap, I32)], axis=1)
    win_start = jnp.minimum(starts // 16 * 16, cap - sw)
    win_n = jnp.maximum((ends - win_start + sw - 1) // sw, 1)
    y = _combine(x1.reshape(n, d), pos3.reshape(N_EXPERTS, n).T, ye, p.reshape(n, PLE_DIM),
                 w["w_ple"], w["w_pg"], w["b_pg"], w["fg"],
                 win_start.T.reshape(-1), win_n.T.reshape(-1), cap)
    return y.reshape(b, l, d)


def kernel(x_prompt, x_sample, p_prompt, p_sample, norm1_g, w_in, conv_k, conv_b, conv_ln_g,
           conv_ln_b, pool_w, pool_scale, w_out, norm2_g, w_router, w_gate, w_up, w_down,
           w_ple, w_pg, b_pg, final_g):
    row = lambda v: v.reshape(1, -1).astype(F32)
    w_r = w_router[0].astype(BF16)
    w = dict(
        g1=row(norm1_g[0]), w_in=w_in[0].astype(BF16), conv_k=conv_k[0].reshape(KERNEL_SIZE, CONV_W),
        conv_b=row(conv_b[0]), ln_g=row(conv_ln_g[0]), ln_b=row(conv_ln_b[0]),
        pool_w=pool_w[0].astype(BF16), pool_scale=row(pool_scale[0]), w_out=w_out[0].astype(BF16),
        g2=row(norm2_g[0]), w_rt=w_r.T, w_r=jnp.pad(w_r, ((0, 0), (0, LANES - N_EXPERTS))),
        w_gate=w_gate[0].astype(BF16), w_up=w_up[0].astype(BF16), w_down=w_down[0].astype(BF16),
        w_ple=w_ple[0].astype(BF16), w_pg=w_pg[0].astype(BF16), b_pg=row(b_pg[0]), fg=row(final_g),
    )
    return (_trunk(x_prompt, p_prompt[0], w), _trunk(x_sample, p_sample[0], w))
```

```python
import functools

import jax
import jax.numpy as jnp
from jax import lax
from jax.experimental import pallas as pl
from jax.experimental.pallas import tpu as pltpu

D_MODEL = 1024
CONV_W = 512
POOL_W = 512
POOL_GROUPS = 4
POOL_GC = 128
POOL_WINDOWS = (2, 4, 8, 16)
KERNEL_SIZE = 31
IN_COLS = 2 * CONV_W + POOL_W
N_EXPERTS = 16
CAPACITY_FACTOR = 2
D_EXPERT = 2 * D_MODEL
PLE_DIM = 256
EPS = 1e-6

LANES = 128
SUBLANES = 8
HALO = 16
VMEM_LIMIT = 56 * 1024 * 1024
ROW_SUB = D_MODEL // LANES

SEQ_TILE = 512
PROJ_BLOCKS = 1
TOKEN_PARTS = 1
SLOT_CHUNK = 256
ROW_GROUP = 4
COMBINE_TILE = 512
SLOT_WINDOW = 128

BF16 = jnp.bfloat16
F32 = jnp.float32
I32 = jnp.int32


def _dot(a, b):
    return jnp.dot(a, b, preferred_element_type=F32)


def _rms(x, g):
    return x * lax.rsqrt(jnp.mean(x * x, axis=-1, keepdims=True) + EPS) * g


def _front_kernel(xp_ref, xm_ref, xn_ref, g1_ref, win_ref, ck_ref, cb_ref, lg_ref, lb_ref,
                  pw_ref, ps_ref, wout_ref, g2_ref, wrt_ref,
                  x1_ref, row_ref, pr_ref, u_ref, zp_ref, *, seq_len, tile):
    t = pl.program_id(1)
    nt = pl.num_programs(1)
    rows = tile + 2 * HALO

    def x_rows(a, b):
        parts = []
        if a < HALO:
            parts.append(xp_ref[a:min(b, HALO), :])
        if max(a, HALO) < min(b, HALO + tile):
            parts.append(xm_ref[max(a, HALO) - HALO:min(b, HALO + tile) - HALO, :])
        if b > HALO + tile:
            parts.append(xn_ref[max(a, HALO + tile) - HALO - tile:b - HALO - tile, :])
        return parts[0] if len(parts) == 1 else jnp.concatenate(parts, axis=0)

    blk = rows // PROJ_BLOCKS
    for i in range(PROJ_BLOCKS):
        a, b = i * blk, (i + 1) * blk
        h = _rms(x_rows(a, b), g1_ref[...]).astype(BF16)
        z = _dot(h, win_ref[...])
        ridx = lax.broadcasted_iota(I32, (blk, LANES), 0) + a
        valid = jnp.logical_and(jnp.logical_or(ridx >= HALO, t > 0),
                                jnp.logical_or(ridx < HALO + tile, t < nt - 1))
        validf = jnp.where(valid, 1.0, 0.0).astype(F32)
        validc = jnp.concatenate([validf] * (CONV_W // LANES), axis=1)
        u_ref[0, a:b, :] = z[:, :CONV_W] * jax.nn.sigmoid(z[:, CONV_W:2 * CONV_W]) * validc
        zp_ref[a:b, :] = z[:, 2 * CONV_W:] * validc
        ca = 0 if i == 0 else a - SUBLANES
        cb = b - SUBLANES
        for r in range(1, SUBLANES):
            u_ref[r, ca:cb, :] = u_ref[0, ca + r:cb + r, :]

    part = tile // TOKEN_PARTS
    for q in range(TOKEN_PARTS):
        q0 = q * part
        conv = None
        for k in range(KERNEL_SIZE):
            off = HALO - KERNEL_SIZE // 2 + k
            r = off % SUBLANES
            term = u_ref[r, pl.ds(q0 + off - r, part), :] * ck_ref[k:k + 1, :]
            conv = term if conv is None else conv + term
        conv = conv + cb_ref[...]
        mu = jnp.mean(conv, axis=-1, keepdims=True)
        cen = conv - mu
        var = jnp.mean(cen * cen, axis=-1, keepdims=True)
        yc = cen * lax.rsqrt(var + EPS) * lg_ref[...] + lb_ref[...]
        yc = yc * jax.nn.sigmoid(yc)

        pos = lax.broadcasted_iota(I32, (part, LANES), 0) + (t * tile + q0)
        yps = []
        for g, w in enumerate(POOL_WINDOWS):
            cols = pl.ds(g * POOL_GC, POOL_GC)
            s = None
            for j in range(-(w // 2), w // 2):
                term = zp_ref[pl.ds(q0 + HALO + j, part), cols]
                s = term if s is None else s + term
            lo = jnp.maximum(pos - w // 2, 0)
            hi = jnp.minimum(pos + w // 2 - 1, seq_len - 1)
            cnt = (hi - lo + 1).astype(F32)
            pooled = s / cnt - zp_ref[pl.ds(q0 + HALO, part), cols]
            yp = _dot(pooled.astype(BF16), pw_ref[g]) * ps_ref[:, g * POOL_GC:(g + 1) * POOL_GC]
            yps.append(yp)

        y = jnp.concatenate([yc] + yps, axis=1).astype(BF16)
        x1 = xm_ref[q0:q0 + part, :] + _dot(y, wout_ref[...])
        x1_ref[q0:q0 + part, :] = x1

        h2f = _rms(x1, g2_ref[...])
        h2 = h2f.astype(BF16)
        logits = lax.dot_general(wrt_ref[...], h2, (((1,), (1,)), ((), ())),
                                 preferred_element_type=F32)
        m = jnp.max(logits, axis=0, keepdims=True)
        ex = jnp.exp(logits - m)
        pr_ref[:, q0:q0 + part] = ex / jnp.sum(ex, axis=0, keepdims=True)
        for s in range(ROW_SUB):
            row_ref[pl.ds(q0 * ROW_SUB + s, part, stride=ROW_SUB), :] = h2f[:, s * LANES:(s + 1) * LANES]


def _front(x, g1, w_in, conv_k, conv_b, ln_g, ln_b, pool_w, pool_scale, w_out, g2, w_rt):
    b, l, d = x.shape
    tile = min(SEQ_TILE, l)
    nt = l // tile
    hpt = tile // HALO
    nh = l // HALO
    n = b * l
    const = lambda shape: pl.BlockSpec(shape, lambda i, j: (0,) * len(shape))
    kern = functools.partial(_front_kernel, seq_len=l, tile=tile)
    return pl.pallas_call(
        kern,
        grid=(b, nt),
        in_specs=[
            pl.BlockSpec((None, HALO, d), lambda i, j: (i, jnp.maximum(j * hpt - 1, 0), 0)),
            pl.BlockSpec((None, tile, d), lambda i, j: (i, j, 0)),
            pl.BlockSpec((None, HALO, d), lambda i, j: (i, jnp.minimum((j + 1) * hpt, nh - 1), 0)),
            const((1, d)), const((d, IN_COLS)), const((KERNEL_SIZE, CONV_W)), const((1, CONV_W)),
            const((1, CONV_W)), const((1, CONV_W)), const((POOL_GROUPS, POOL_GC, POOL_GC)),
            const((1, POOL_W)), const((d, d)), const((1, d)), const((N_EXPERTS, d)),
        ],
        out_specs=[
            pl.BlockSpec((None, tile, d), lambda i, j: (i, j, 0)),
            pl.BlockSpec((tile * ROW_SUB, LANES), lambda i, j: (i * nt + j, 0)),
            pl.BlockSpec((N_EXPERTS, tile), lambda i, j: (0, i * nt + j)),
        ],
        out_shape=[
            jax.ShapeDtypeStruct((b, l, d), F32),
            jax.ShapeDtypeStruct((n * ROW_SUB, LANES), F32),
            jax.ShapeDtypeStruct((N_EXPERTS, n), F32),
        ],
        scratch_shapes=[pltpu.VMEM((SUBLANES, tile + 2 * HALO, CONV_W), F32),
                        pltpu.VMEM((tile + 2 * HALO, POOL_W), F32)],
        compiler_params=pltpu.CompilerParams(
            dimension_semantics=("arbitrary", "arbitrary"), vmem_limit_bytes=VMEM_LIMIT),
        name="front",
    )(x, x, x, g1, w_in, conv_k, conv_b, ln_g, ln_b, pool_w, pool_scale, w_out, g2, w_rt)


def _select_kernel(p_ref, pos_ref, rank_ref, off_ref, lo_ref, hi_ref, tot_ref, *, cap, chunk):
    e, r, _ = p_ref.shape
    p = p_ref[...]
    capf = jnp.float32(cap)

    def count_ge(thr):
        hit = jnp.where(p >= thr, 1.0, 0.0).astype(F32)
        return jnp.sum(jnp.sum(hit, axis=1, keepdims=True), axis=2, keepdims=True)

    def bisect(i, bits):
        cand = jnp.bitwise_or(bits, jnp.left_shift(jnp.int32(1), 30 - i))
        return jnp.where(count_ge(lax.bitcast_convert_type(cand, F32)) >= capf, cand, bits)

    tau_bits = lax.fori_loop(0, 31, bisect, jnp.zeros((e, 1, 1), I32))
    tau = lax.bitcast_convert_type(tau_bits, F32)
    above_tau = p >= lax.bitcast_convert_type(tau_bits + 1, F32)
    at_tau = jnp.logical_and(p >= tau, jnp.logical_not(above_tau))

    gt = jnp.where(above_tau, 1.0, 0.0).astype(BF16).reshape(e * r, LANES)
    eq = jnp.where(at_tau, 1.0, 0.0).astype(BF16).reshape(e * r, LANES)
    ri = lax.broadcasted_iota(I32, (LANES, LANES), 0)
    ci = lax.broadcasted_iota(I32, (LANES, LANES), 1)
    upper = jnp.where(ri <= ci, 1.0, 0.0).astype(BF16)
    ones = jnp.ones((LANES, LANES), BF16)
    rr = lax.broadcasted_iota(I32, (r, r), 0)
    rc = lax.broadcasted_iota(I32, (r, r), 1)
    lower = jnp.where(rc < rr, 1.0, 0.0).astype(BF16)

    def prefix(mask):
        within = _dot(mask, upper).reshape(e, r, LANES)
        tot_ref[...] = _dot(mask, ones).reshape(e, r, LANES)
        offs = [_dot(lower, tot_ref[i].astype(BF16)) for i in range(e)]
        return within, jnp.stack(offs, axis=0), tot_ref[...]

    gt_in, gt_off, gt_tot = prefix(gt)
    eq_in, eq_off, eq_tot = prefix(eq)
    n_gt = gt_off[:, r - 1:r, :] + gt_tot[:, r - 1:r, :]
    need = capf - n_gt
    eq_incl = eq_in + eq_off
    sel = jnp.logical_or(above_tau, jnp.logical_and(at_tau, eq_incl <= need))
    rank = gt_in + gt_off + jnp.minimum(eq_incl, need)
    rank_ref[...] = rank.astype(I32)
    pos_ref[...] = jnp.where(sel, rank - 1.0, -1.0).astype(I32)
    off_ref[...] = (gt_off + jnp.minimum(eq_off, need)).astype(I32)
    row_end = gt_off + gt_tot + jnp.minimum(eq_off + eq_tot, need)
    first = (lax.broadcasted_iota(I32, (1, 1, LANES), 2) * chunk).astype(F32)
    lo_ref[...] = jnp.sum(jnp.where(row_end <= first, 1.0, 0.0), axis=1, keepdims=True).astype(I32)
    hi_ref[...] = jnp.sum(jnp.where(row_end <= first + (chunk - 1.0), 1.0, 0.0),
                          axis=1, keepdims=True).astype(I32)


def _select(probs3, cap, chunk):
    e, r, _ = probs3.shape
    full = lambda s: pl.BlockSpec(s, lambda: (0,) * len(s))
    big = jax.ShapeDtypeStruct((e, r, LANES), I32)
    small = jax.ShapeDtypeStruct((e, 1, LANES), I32)
    return pl.pallas_call(
        functools.partial(_select_kernel, cap=cap, chunk=chunk),
        in_specs=[full((e, r, LANES))],
        out_specs=[full((e, r, LANES))] * 3 + [full((e, 1, LANES))] * 2,
        out_shape=[big, big, big, small, small],
        scratch_shapes=[pltpu.VMEM((e, r, LANES), F32)],
        compiler_params=pltpu.CompilerParams(vmem_limit_bytes=VMEM_LIMIT),
        name="select",
    )(probs3)


def _invert_kernel(lo_ref, hi_ref, rank_ref, idx_ref, *, n_chunks, chunk):
    e = pl.program_id(0)

    def one_chunk(c, carry):
        lo = lo_ref[e * n_chunks + c] // ROW_GROUP
        hi = hi_ref[e * n_chunks + c] // ROW_GROUP
        slot = lax.broadcasted_iota(I32, (chunk, LANES), 0) + c * chunk

        def one_group(g, acc):
            for j in range(ROW_GROUP):
                acc = acc + jnp.where(rank_ref[pl.ds(g * ROW_GROUP + j, 1), :] <= slot, 1.0, 0.0)
            return acc

        acc = lax.fori_loop(lo, hi + 1, one_group, jnp.zeros((chunk, LANES), F32))
        cnt = jnp.sum(acc.T, axis=0, keepdims=True)
        idx_ref[pl.ds(c, 1), :] = cnt.astype(I32) + lo * (ROW_GROUP * LANES)
        return carry

    lax.fori_loop(0, n_chunks, one_chunk, 0)


def _invert(rank3, lo_row, hi_row, n_chunks, chunk):
    e, r, _ = rank3.shape
    grid_spec = pltpu.PrefetchScalarGridSpec(
        num_scalar_prefetch=2,
        grid=(e,),
        in_specs=[pl.BlockSpec((None, r, LANES), lambda i, *_: (i, 0, 0))],
        out_specs=pl.BlockSpec((None, n_chunks, chunk), lambda i, *_: (i, 0, 0)),
    )
    return pl.pallas_call(
        functools.partial(_invert_kernel, n_chunks=n_chunks, chunk=chunk),
        grid_spec=grid_spec,
        out_shape=jax.ShapeDtypeStruct((e, n_chunks, chunk), I32),
        compiler_params=pltpu.CompilerParams(dimension_semantics=("arbitrary",)),
        name="invert",
    )(lo_row, hi_row, rank3)


def _experts_kernel(cur_ref, nxt_ref, row_hbm, wg_ref, wu_ref, wd_ref, ye_ref, buf, sem,
                    xe_ref, *, n_chunks, chunk):
    e = pl.program_id(0)
    step = e * n_chunks + pl.program_id(1)
    n_steps = N_EXPERTS * n_chunks
    slot = lax.rem(step, 2)

    def gather(idx_ref, s):
        for r in range(chunk):
            src = pl.multiple_of(idx_ref[0, r] * ROW_SUB, ROW_SUB)
            pltpu.make_async_copy(row_hbm.at[pl.ds(src, ROW_SUB), :],
                                  buf.at[s, pl.ds(r * ROW_SUB, ROW_SUB), :], sem.at[s]).start()

    def wait(s):
        pltpu.make_async_copy(row_hbm.at[pl.ds(0, chunk * ROW_SUB), :], buf.at[s], sem.at[s]).wait()

    @pl.when(step == 0)
    def _():
        gather(cur_ref, 0)

    wait(slot)
    for s in range(ROW_SUB):
        xe_ref[:, s * LANES:(s + 1) * LANES] = buf[slot, pl.ds(s, chunk, stride=ROW_SUB), :].astype(BF16)
    xe = xe_ref[...]
    a = _dot(xe, wg_ref[...])
    b = _dot(xe, wu_ref[...])
    hid = (a * jax.nn.sigmoid(a) * b).astype(BF16)
    ye_ref[...] = _dot(hid, wd_ref[...]).astype(BF16)
    gather(nxt_ref, 1 - slot)

    @pl.when(step == n_steps - 1)
    def _():
        wait(1 - slot)


def _experts(rows, idx, wg, wu, wd, cap):
    d = D_MODEL
    e, n_chunks, chunk = idx.shape
    n_steps = e * n_chunks
    idx = idx.reshape(n_steps, 1, chunk)
    smem = lambda f: pl.BlockSpec((None, 1, chunk), f, memory_space=pltpu.SMEM)
    return pl.pallas_call(
        functools.partial(_experts_kernel, n_chunks=n_chunks, chunk=chunk),
        grid=(e, n_chunks),
        in_specs=[
            smem(lambda i, c: (i * n_chunks + c, 0, 0)),
            smem(lambda i, c: (jnp.minimum(i * n_chunks + c + 1, n_steps - 1), 0, 0)),
            pl.BlockSpec(memory_space=pl.ANY),
            pl.BlockSpec((None, d, D_EXPERT), lambda i, c: (i, 0, 0)),
            pl.BlockSpec((None, d, D_EXPERT), lambda i, c: (i, 0, 0)),
            pl.BlockSpec((None, D_EXPERT, d), lambda i, c: (i, 0, 0)),
        ],
        out_specs=pl.BlockSpec((None, chunk, d), lambda i, c: (i, c, 0)),
        out_shape=jax.ShapeDtypeStruct((e, cap, d), BF16),
        scratch_shapes=[pltpu.VMEM((2, chunk * ROW_SUB, LANES), F32), pltpu.SemaphoreType.DMA((2,)),
                        pltpu.VMEM((chunk, D_MODEL), BF16)],
        compiler_params=pltpu.CompilerParams(
            dimension_semantics=("arbitrary", "arbitrary"), vmem_limit_bytes=VMEM_LIMIT),
        name="experts",
    )(idx, idx, rows, wg, wu, wd)


def _combine_kernel(ws_ref, wn_ref, x1_ref, pos_ref, pr_ref, ye_hbm, p_ref, wple_ref, wpg_ref, bpg_ref,
                    fg_ref, y_ref, ybuf, sem, xbuf, xsem, acc_ref, *, cap, tile, sw):
    i = pl.program_id(0)
    n = pl.num_programs(0)
    slot = lax.rem(i, 2)

    def copies(step, s):
        return [pltpu.make_async_copy(
            ye_hbm.at[pl.ds(pl.multiple_of(e * cap + ws_ref[step * N_EXPERTS + e], 16), sw), :],
            ybuf.at[s, pl.ds(e * sw, sw), :], sem.at[s]) for e in range(N_EXPERTS)]

    @pl.when(i == 0)
    def _():
        for cp in copies(0, 0):
            cp.start()

    @pl.when(i + 1 < n)
    def _():
        for cp in copies(i + 1, 1 - slot):
            cp.start()

    for cp in copies(i, slot):
        cp.wait()

    pos = pos_ref[...]
    gates = pr_ref[...]
    lane = lax.broadcasted_iota(I32, (tile, sw), 1)
    hits = []
    for e in range(N_EXPERTS):
        rel = pos[:, e:e + 1] - ws_ref[i * N_EXPERTS + e]
        hits.append(jnp.where(rel == lane, gates[:, e:e + 1], 0.0).astype(BF16))
    acc_ref[...] = _dot(jnp.concatenate(hits, axis=1), ybuf[slot])

    for e in range(N_EXPERTS):
        first = ws_ref[i * N_EXPERTS + e]

        def extra(w, carry, e=e, first=first):
            want = first + w * sw
            start = pl.multiple_of(jnp.minimum(want, cap - sw), 16)
            cp = pltpu.make_async_copy(ye_hbm.at[pl.ds(e * cap + start, sw), :], xbuf, xsem)
            cp.start()
            cp.wait()
            col = pos[:, e:e + 1]
            hit = jnp.logical_and(col - start == lane, col >= want)
            acc_ref[...] += _dot(jnp.where(hit, gates[:, e:e + 1], 0.0).astype(BF16), xbuf[...])
            return carry

        lax.fori_loop(1, wn_ref[i * N_EXPERTS + e], extra, 0)

    x2 = x1_ref[...] + acc_ref[...]
    gate = jax.nn.sigmoid(_dot(x2.astype(BF16), wpg_ref[...]) + bpg_ref[...])
    x3 = x2 + gate * _dot(p_ref[...].astype(BF16), wple_ref[...])
    y_ref[...] = _rms(x3, fg_ref[...])


def _combine(x1, pos_tok, gate_tok, ye, p, w_ple, w_pg, b_pg, fg, win_start, win_n, cap):
    n, d = x1.shape
    tile = min(COMBINE_TILE, n)
    sw = min(SLOT_WINDOW, cap)
    const = lambda shape: pl.BlockSpec(shape, lambda i, *_: (0,) * len(shape))
    grid_spec = pltpu.PrefetchScalarGridSpec(
        num_scalar_prefetch=2,
        grid=(n // tile,),
        in_specs=[
            pl.BlockSpec((tile, d), lambda i, *_: (i, 0)),
            pl.BlockSpec((tile, N_EXPERTS), lambda i, *_: (i, 0)),
            pl.BlockSpec((tile, N_EXPERTS), lambda i, *_: (i, 0)),
            pl.BlockSpec(memory_space=pl.ANY),
            pl.BlockSpec((tile, PLE_DIM), lambda i, *_: (i, 0)),
            const((PLE_DIM, d)), const((d, d)), const((1, d)), const((1, d)),
        ],
        out_specs=pl.BlockSpec((tile, d), lambda i, *_: (i, 0)),
        scratch_shapes=[pltpu.VMEM((2, N_EXPERTS * sw, d), BF16), pltpu.SemaphoreType.DMA((2,)),
                        pltpu.VMEM((sw, d), BF16), pltpu.SemaphoreType.DMA(()),
                        pltpu.VMEM((tile, d), F32)],
    )
    return pl.pallas_call(
        functools.partial(_combine_kernel, cap=cap, tile=tile, sw=sw),
        grid_spec=grid_spec,
        out_shape=jax.ShapeDtypeStruct((n, d), F32),
        compiler_params=pltpu.CompilerParams(
            dimension_semantics=("arbitrary",), vmem_limit_bytes=VMEM_LIMIT),
        name="combine",
    )(win_start, win_n, x1, pos_tok, gate_tok, ye.reshape(N_EXPERTS * cap, d), p, w_ple, w_pg, b_pg, fg)


def _trunk(x, p, w):
    b, l, d = x.shape
    n = b * l
    cap = CAPACITY_FACTOR * n // N_EXPERTS
    rows = n // LANES

    x1, tok_rows, probs = _front(x, w["g1"], w["w_in"], w["conv_k"], w["conv_b"], w["ln_g"],
                                 w["ln_b"], w["pool_w"], w["pool_scale"], w["w_out"], w["g2"],
                                 w["w_rt"])

    chunk = min(SLOT_CHUNK, cap)
    n_chunks = cap // chunk
    pos3, rank3, off3, lo3, hi3 = _select(probs.reshape(N_EXPERTS, rows, LANES), cap, chunk)

    lo_row = lo3[:, 0, :n_chunks].reshape(-1)
    hi_row = hi3[:, 0, :n_chunks].reshape(-1)
    idx = _invert(rank3, lo_row, hi_row, n_chunks, chunk)
    ye = _experts(tok_rows, idx, w["w_gate"], w["w_up"], w["w_down"], cap)

    tile = min(COMBINE_TILE, n)
    sw = min(SLOT_WINDOW, cap)
    starts = off3[:, ::tile // LANES, 0]
    ends = jnp.concatenate([starts[:, 1:], jnp.full((N_EXPERTS, 1), cap, I32)], axis=1)
    win_start = jnp.minimum(starts // 16 * 16, cap - sw)
    win_n = jnp.maximum((ends - win_start + sw - 1) // sw, 1)
    y = _combine(x1.reshape(n, d), pos3.reshape(N_EXPERTS, n).T, probs.T, ye, p.reshape(n, PLE_DIM),
                 w["w_ple"], w["w_pg"], w["b_pg"], w["fg"],
                 win_start.T.reshape(-1), win_n.T.reshape(-1), cap)
    return y.reshape(b, l, d)


def kernel(x_prompt, x_sample, p_prompt, p_sample, norm1_g, w_in, conv_k, conv_b, conv_ln_g,
           conv_ln_b, pool_w, pool_scale, w_out, norm2_g, w_router, w_gate, w_up, w_down,
           w_ple, w_pg, b_pg, final_g):
    row = lambda v: v.reshape(1, -1).astype(F32)
    w = dict(
        g1=row(norm1_g[0]), w_in=w_in[0].astype(BF16), conv_k=conv_k[0].reshape(KERNEL_SIZE, CONV_W),
        conv_b=row(conv_b[0]), ln_g=row(conv_ln_g[0]), ln_b=row(conv_ln_b[0]),
        pool_w=pool_w[0].astype(BF16), pool_scale=row(pool_scale[0]), w_out=w_out[0].astype(BF16),
        g2=row(norm2_g[0]), w_rt=w_router[0].T.astype(BF16),
        w_gate=w_gate[0].astype(BF16), w_up=w_up[0].astype(BF16), w_down=w_down[0].astype(BF16),
        w_ple=w_ple[0].astype(BF16), w_pg=w_pg[0].astype(BF16), b_pg=row(b_pg[0]), fg=row(final_g),
    )
    return (_trunk(x_prompt, p_prompt[0], w), _trunk(x_sample, p_sample[0], w))
```

```python
import functools

import jax
import jax.numpy as jnp
from jax import lax
from jax.experimental import pallas as pl
from jax.experimental.pallas import tpu as pltpu

D_MODEL = 1024
CONV_W = 512
POOL_W = 512
POOL_GROUPS = 4
POOL_GC = 128
POOL_WINDOWS = (2, 4, 8, 16)
KERNEL_SIZE = 31
IN_COLS = 2 * CONV_W + POOL_W
N_EXPERTS = 16
CAPACITY_FACTOR = 2
D_EXPERT = 2 * D_MODEL
PLE_DIM = 256
EPS = 1e-6

LANES = 128
SUBLANES = 8
HALO = 16
VMEM_LIMIT = 56 * 1024 * 1024
ROW_SUB = D_MODEL // LANES

SEQ_TILE = 512
SLOT_CHUNK = 256
ROW_GROUP = 4
COMBINE_TILE = 512
SLOT_WINDOW = 128

BF16 = jnp.bfloat16
F32 = jnp.float32
I32 = jnp.int32


def _dot(a, b):
    return jnp.dot(a, b, preferred_element_type=F32)


def _rms(x, g):
    return x * lax.rsqrt(jnp.mean(x * x, axis=-1, keepdims=True) + EPS) * g


def _front_kernel(xp_ref, xm_ref, xn_ref, g1_ref, win_ref, ck_ref, cb_ref, lg_ref, lb_ref,
                  pw_ref, ps_ref, wout_ref, g2_ref, wrt_ref,
                  x1_ref, row_ref, pr_ref, u_ref, zp_ref, *, seq_len, tile):
    t = pl.program_id(1)
    nt = pl.num_programs(1)
    rows = tile + 2 * HALO

    xs = jnp.concatenate([xp_ref[...], xm_ref[...], xn_ref[...]], axis=0)
    h = _rms(xs, g1_ref[...]).astype(BF16)
    z = _dot(h, win_ref[...])

    u_ref[0] = z[:, :CONV_W] * jax.nn.sigmoid(z[:, CONV_W:2 * CONV_W])
    zp_ref[...] = z[:, 2 * CONV_W:]
    for first, keep in ((0, t > 0), (HALO + tile, t < nt - 1)):
        edge = pl.ds(first, HALO)
        scale = jnp.where(keep, 1.0, 0.0).astype(F32)
        u_ref[0, edge, :] = u_ref[0, edge, :] * scale
        zp_ref[edge, :] = zp_ref[edge, :] * scale
    for r in range(1, SUBLANES):
        u_ref[r, pl.ds(0, rows - SUBLANES), :] = u_ref[0, pl.ds(r, rows - SUBLANES), :]

    conv = None
    for k in range(KERNEL_SIZE):
        off = HALO - KERNEL_SIZE // 2 + k
        r = off % SUBLANES
        term = u_ref[r, pl.ds(off - r, tile), :] * ck_ref[k:k + 1, :]
        conv = term if conv is None else conv + term
    conv = conv + cb_ref[...]
    mu = jnp.mean(conv, axis=-1, keepdims=True)
    cen = conv - mu
    var = jnp.mean(cen * cen, axis=-1, keepdims=True)
    yc = cen * lax.rsqrt(var + EPS) * lg_ref[...] + lb_ref[...]
    yc = yc * jax.nn.sigmoid(yc)

    pos = lax.broadcasted_iota(I32, (tile, LANES), 0) + t * tile
    yps = []
    for g, w in enumerate(POOL_WINDOWS):
        cols = pl.ds(g * POOL_GC, POOL_GC)
        s = None
        for j in range(-(w // 2), w // 2):
            term = zp_ref[pl.ds(HALO + j, tile), cols]
            s = term if s is None else s + term
        lo = jnp.maximum(pos - w // 2, 0)
        hi = jnp.minimum(pos + w // 2 - 1, seq_len - 1)
        cnt = (hi - lo + 1).astype(F32)
        pooled = s / cnt - zp_ref[pl.ds(HALO, tile), cols]
        yp = _dot(pooled.astype(BF16), pw_ref[g]) * ps_ref[:, g * POOL_GC:(g + 1) * POOL_GC]
        yps.append(yp)

    y = jnp.concatenate([yc] + yps, axis=1).astype(BF16)
    x1 = xm_ref[...] + _dot(y, wout_ref[...])
    x1_ref[...] = x1

    h2f = _rms(x1, g2_ref[...])
    h2 = h2f.astype(BF16)
    logits = lax.dot_general(wrt_ref[...], h2, (((1,), (1,)), ((), ())),
                             preferred_element_type=F32)
    m = jnp.max(logits, axis=0, keepdims=True)
    ex = jnp.exp(logits - m)
    pr_ref[...] = ex / jnp.sum(ex, axis=0, keepdims=True)
    for s in range(ROW_SUB):
        row_ref[pl.ds(s, tile, stride=ROW_SUB), :] = h2f[:, s * LANES:(s + 1) * LANES]


def _front(x, g1, w_in, conv_k, conv_b, ln_g, ln_b, pool_w, pool_scale, w_out, g2, w_rt):
    b, l, d = x.shape
    tile = min(SEQ_TILE, l)
    nt = l // tile
    hpt = tile // HALO
    nh = l // HALO
    n = b * l
    const = lambda shape: pl.BlockSpec(shape, lambda i, j: (0,) * len(shape))
    kern = functools.partial(_front_kernel, seq_len=l, tile=tile)
    return pl.pallas_call(
        kern,
        grid=(b, nt),
        in_specs=[
            pl.BlockSpec((None, HALO, d), lambda i, j: (i, jnp.maximum(j * hpt - 1, 0), 0)),
            pl.BlockSpec((None, tile, d), lambda i, j: (i, j, 0)),
            pl.BlockSpec((None, HALO, d), lambda i, j: (i, jnp.minimum((j + 1) * hpt, nh - 1), 0)),
            const((1, d)), const((d, IN_COLS)), const((KERNEL_SIZE, CONV_W)), const((1, CONV_W)),
            const((1, CONV_W)), const((1, CONV_W)), const((POOL_GROUPS, POOL_GC, POOL_GC)),
            const((1, POOL_W)), const((d, d)), const((1, d)), const((N_EXPERTS, d)),
        ],
        out_specs=[
            pl.BlockSpec((None, tile, d), lambda i, j: (i, j, 0)),
            pl.BlockSpec((tile * ROW_SUB, LANES), lambda i, j: (i * nt + j, 0)),
            pl.BlockSpec((N_EXPERTS, tile), lambda i, j: (0, i * nt + j)),
        ],
        out_shape=[
            jax.ShapeDtypeStruct((b, l, d), F32),
            jax.ShapeDtypeStruct((n * ROW_SUB, LANES), F32),
            jax.ShapeDtypeStruct((N_EXPERTS, n), F32),
        ],
        scratch_shapes=[pltpu.VMEM((SUBLANES, tile + 2 * HALO, CONV_W), F32),
                        pltpu.VMEM((tile + 2 * HALO, POOL_W), F32)],
        compiler_params=pltpu.CompilerParams(
            dimension_semantics=("arbitrary", "arbitrary"), vmem_limit_bytes=VMEM_LIMIT),
        name="front",
    )(x, x, x, g1, w_in, conv_k, conv_b, ln_g, ln_b, pool_w, pool_scale, w_out, g2, w_rt)


def _select_kernel(p_ref, pos_ref, rank_ref, off_ref, lo_ref, hi_ref, tot_ref, *, cap, chunk):
    e, r, _ = p_ref.shape
    p = p_ref[...]
    capf = jnp.float32(cap)

    def count_ge(thr):
        hit = jnp.where(p >= thr, 1.0, 0.0).astype(F32)
        return jnp.sum(jnp.sum(hit, axis=1, keepdims=True), axis=2, keepdims=True)

    def bisect(i, bits):
        cand = jnp.bitwise_or(bits, jnp.left_shift(jnp.int32(1), 30 - i))
        return jnp.where(count_ge(lax.bitcast_convert_type(cand, F32)) >= capf, cand, bits)

    tau_bits = lax.fori_loop(0, 31, bisect, jnp.zeros((e, 1, 1), I32))
    tau = lax.bitcast_convert_type(tau_bits, F32)
    above_tau = p >= lax.bitcast_convert_type(tau_bits + 1, F32)
    at_tau = jnp.logical_and(p >= tau, jnp.logical_not(above_tau))

    gt = jnp.where(above_tau, 1.0, 0.0).astype(BF16).reshape(e * r, LANES)
    eq = jnp.where(at_tau, 1.0, 0.0).astype(BF16).reshape(e * r, LANES)
    ri = lax.broadcasted_iota(I32, (LANES, LANES), 0)
    ci = lax.broadcasted_iota(I32, (LANES, LANES), 1)
    upper = jnp.where(ri <= ci, 1.0, 0.0).astype(BF16)
    ones = jnp.ones((LANES, LANES), BF16)
    rr = lax.broadcasted_iota(I32, (r, r), 0)
    rc = lax.broadcasted_iota(I32, (r, r), 1)
    lower = jnp.where(rc < rr, 1.0, 0.0).astype(BF16)

    def prefix(mask):
        within = _dot(mask, upper).reshape(e, r, LANES)
        tot_ref[...] = _dot(mask, ones).reshape(e, r, LANES)
        offs = [_dot(lower, tot_ref[i].astype(BF16)) for i in range(e)]
        return within, jnp.stack(offs, axis=0), tot_ref[...]

    gt_in, gt_off, gt_tot = prefix(gt)
    eq_in, eq_off, eq_tot = prefix(eq)
    n_gt = gt_off[:, r - 1:r, :] + gt_tot[:, r - 1:r, :]
    need = capf - n_gt
    eq_incl = eq_in + eq_off
    sel = jnp.logical_or(above_tau, jnp.logical_and(at_tau, eq_incl <= need))
    rank = gt_in + gt_off + jnp.minimum(eq_incl, need)
    rank_ref[...] = rank.astype(I32)
    pos_ref[...] = jnp.where(sel, rank - 1.0, -1.0).astype(I32)
    off_ref[...] = (gt_off + jnp.minimum(eq_off, need)).astype(I32)
    row_end = gt_off + gt_tot + jnp.minimum(eq_off + eq_tot, need)
    first = (lax.broadcasted_iota(I32, (1, 1, LANES), 2) * chunk).astype(F32)
    lo_ref[...] = jnp.sum(jnp.where(row_end <= first, 1.0, 0.0), axis=1, keepdims=True).astype(I32)
    hi_ref[...] = jnp.sum(jnp.where(row_end <= first + (chunk - 1.0), 1.0, 0.0),
                          axis=1, keepdims=True).astype(I32)


def _select(probs3, cap, chunk):
    e, r, _ = probs3.shape
    full = lambda s: pl.BlockSpec(s, lambda: (0,) * len(s))
    big = jax.ShapeDtypeStruct((e, r, LANES), I32)
    small = jax.ShapeDtypeStruct((e, 1, LANES), I32)
    return pl.pallas_call(
        functools.partial(_select_kernel, cap=cap, chunk=chunk),
        in_specs=[full((e, r, LANES))],
        out_specs=[full((e, r, LANES))] * 3 + [full((e, 1, LANES))] * 2,
        out_shape=[big, big, big, small, small],
        scratch_shapes=[pltpu.VMEM((e, r, LANES), F32)],
        compiler_params=pltpu.CompilerParams(vmem_limit_bytes=VMEM_LIMIT),
        name="select",
    )(probs3)


def _invert_kernel(lo_ref, hi_ref, rank_ref, idx_ref, *, n_chunks, chunk):
    e = pl.program_id(0)

    def one_chunk(c, carry):
        lo = lo_ref[e * n_chunks + c] // ROW_GROUP
        hi = hi_ref[e * n_chunks + c] // ROW_GROUP
        slot = lax.broadcasted_iota(I32, (chunk, LANES), 0) + c * chunk

        def one_group(g, acc):
            for j in range(ROW_GROUP):
                acc = acc + jnp.where(rank_ref[pl.ds(g * ROW_GROUP + j, 1), :] <= slot, 1.0, 0.0)
            return acc

        acc = lax.fori_loop(lo, hi + 1, one_group, jnp.zeros((chunk, LANES), F32))
        cnt = jnp.sum(acc.T, axis=0, keepdims=True)
        idx_ref[pl.ds(c, 1), :] = cnt.astype(I32) + lo * (ROW_GROUP * LANES)
        return carry

    lax.fori_loop(0, n_chunks, one_chunk, 0)


def _invert(rank3, lo_row, hi_row, n_chunks, chunk):
    e, r, _ = rank3.shape
    grid_spec = pltpu.PrefetchScalarGridSpec(
        num_scalar_prefetch=2,
        grid=(e,),
        in_specs=[pl.BlockSpec((None, r, LANES), lambda i, *_: (i, 0, 0))],
        out_specs=pl.BlockSpec((None, n_chunks, chunk), lambda i, *_: (i, 0, 0)),
    )
    return pl.pallas_call(
        functools.partial(_invert_kernel, n_chunks=n_chunks, chunk=chunk),
        grid_spec=grid_spec,
        out_shape=jax.ShapeDtypeStruct((e, n_chunks, chunk), I32),
        compiler_params=pltpu.CompilerParams(dimension_semantics=("arbitrary",)),
        name="invert",
    )(lo_row, hi_row, rank3)


def _experts_kernel(cur_ref, nxt_ref, row_hbm, wg_ref, wu_ref, wd_ref, ye_ref, buf, sem,
                    xe_ref, *, n_chunks, chunk):
    step = pl.program_id(0) * n_chunks + pl.program_id(1)
    n_steps = N_EXPERTS * n_chunks
    slot = lax.rem(step, 2)

    def gather(idx_ref, s):
        for r in range(chunk):
            src = pl.multiple_of(idx_ref[0, r] * ROW_SUB, ROW_SUB)
            pltpu.make_async_copy(row_hbm.at[pl.ds(src, ROW_SUB), :],
                                  buf.at[s, pl.ds(r * ROW_SUB, ROW_SUB), :], sem.at[s]).start()

    def wait(s):
        pltpu.make_async_copy(row_hbm.at[pl.ds(0, chunk * ROW_SUB), :], buf.at[s], sem.at[s]).wait()

    @pl.when(step == 0)
    def _():
        gather(cur_ref, 0)

    wait(slot)
    gather(nxt_ref, 1 - slot)
    for k in range(ROW_SUB):
        xe_ref[:, k * LANES:(k + 1) * LANES] = buf[slot, pl.ds(k, chunk, stride=ROW_SUB), :].astype(BF16)
    xe = xe_ref[...]
    a = _dot(xe, wg_ref[...])
    b = _dot(xe, wu_ref[...])
    hid = (a * jax.nn.sigmoid(a) * b).astype(BF16)
    ye_ref[...] = _dot(hid, wd_ref[...]).astype(BF16)

    @pl.when(step == n_steps - 1)
    def _():
        wait(1 - slot)


def _experts(rows, idx, wg, wu, wd, cap):
    d = D_MODEL
    e, n_chunks, chunk = idx.shape
    n_steps = e * n_chunks
    idx = idx.reshape(n_steps, 1, chunk)
    smem = lambda f: pl.BlockSpec((None, 1, chunk), f, memory_space=pltpu.SMEM)
    return pl.pallas_call(
        functools.partial(_experts_kernel, n_chunks=n_chunks, chunk=chunk),
        grid=(e, n_chunks),
        in_specs=[
            smem(lambda i, c: (i * n_chunks + c, 0, 0)),
            smem(lambda i, c: (jnp.minimum(i * n_chunks + c + 1, n_steps - 1), 0, 0)),
            pl.BlockSpec(memory_space=pl.ANY),
            pl.BlockSpec((None, d, D_EXPERT), lambda i, c: (i, 0, 0)),
            pl.BlockSpec((None, d, D_EXPERT), lambda i, c: (i, 0, 0)),
            pl.BlockSpec((None, D_EXPERT, d), lambda i, c: (i, 0, 0)),
        ],
        out_specs=pl.BlockSpec((None, chunk, d), lambda i, c: (i, c, 0)),
        out_shape=jax.ShapeDtypeStruct((e, cap, d), BF16),
        scratch_shapes=[pltpu.VMEM((2, chunk * ROW_SUB, LANES), F32), pltpu.SemaphoreType.DMA((2,)),
                        pltpu.VMEM((chunk, D_MODEL), BF16)],
        compiler_params=pltpu.CompilerParams(
            dimension_semantics=("arbitrary", "arbitrary"), vmem_limit_bytes=VMEM_LIMIT),
        name="experts",
    )(idx, idx, rows, wg, wu, wd)


def _combine_kernel(ws_ref, wn_ref, x1_ref, pos_ref, pr_ref, ye_hbm, p_ref, wple_ref, wpg_ref, bpg_ref,
                    fg_ref, y_ref, ybuf, sem, xbuf, xsem, acc_ref, *, cap, tile, sw):
    i = pl.program_id(0)
    n = pl.num_programs(0)
    slot = lax.rem(i, 2)

    def copies(step, s):
        return [pltpu.make_async_copy(
            ye_hbm.at[pl.ds(pl.multiple_of(e * cap + ws_ref[step * N_EXPERTS + e], 16), sw), :],
            ybuf.at[s, pl.ds(e * sw, sw), :], sem.at[s]) for e in range(N_EXPERTS)]

    @pl.when(i == 0)
    def _():
        for cp in copies(0, 0):
            cp.start()

    @pl.when(i + 1 < n)
    def _():
        for cp in copies(i + 1, 1 - slot):
            cp.start()

    for cp in copies(i, slot):
        cp.wait()

    pos = pos_ref[...]
    gates = pr_ref[...]
    lane = lax.broadcasted_iota(I32, (tile, sw), 1)
    hits = []
    for e in range(N_EXPERTS):
        rel = pos[:, e:e + 1] - ws_ref[i * N_EXPERTS + e]
        hits.append(jnp.where(rel == lane, gates[:, e:e + 1], 0.0).astype(BF16))
    acc_ref[...] = _dot(jnp.concatenate(hits, axis=1), ybuf[slot])

    for e in range(N_EXPERTS):
        first = ws_ref[i * N_EXPERTS + e]

        def extra(w, carry, e=e, first=first):
            want = first + w * sw
            start = pl.multiple_of(jnp.minimum(want, cap - sw), 16)
            cp = pltpu.make_async_copy(ye_hbm.at[pl.ds(e * cap + start, sw), :], xbuf, xsem)
            cp.start()
            cp.wait()
            col = pos[:, e:e + 1]
            hit = jnp.logical_and(col - start == lane, col >= want)
            acc_ref[...] += _dot(jnp.where(hit, gates[:, e:e + 1], 0.0).astype(BF16), xbuf[...])
            return carry

        lax.fori_loop(1, wn_ref[i * N_EXPERTS + e], extra, 0)

    x2 = x1_ref[...] + acc_ref[...]
    gate = jax.nn.sigmoid(_dot(x2.astype(BF16), wpg_ref[...]) + bpg_ref[...])
    x3 = x2 + gate * _dot(p_ref[...].astype(BF16), wple_ref[...])
    y_ref[...] = _rms(x3, fg_ref[...])


def _combine(x1, pos_tok, gate_tok, ye, p, w_ple, w_pg, b_pg, fg, win_start, win_n, cap):
    n, d = x1.shape
    tile = min(COMBINE_TILE, n)
    sw = min(SLOT_WINDOW, cap)
    const = lambda shape: pl.BlockSpec(shape, lambda i, *_: (0,) * len(shape))
    grid_spec = pltpu.PrefetchScalarGridSpec(
        num_scalar_prefetch=2,
        grid=(n // tile,),
        in_specs=[
            pl.BlockSpec((tile, d), lambda i, *_: (i, 0)),
            pl.BlockSpec((tile, N_EXPERTS), lambda i, *_: (i, 0)),
            pl.BlockSpec((tile, N_EXPERTS), lambda i, *_: (i, 0)),
            pl.BlockSpec(memory_space=pl.ANY),
            pl.BlockSpec((tile, PLE_DIM), lambda i, *_: (i, 0)),
            const((PLE_DIM, d)), const((d, d)), const((1, d)), const((1, d)),
        ],
        out_specs=pl.BlockSpec((tile, d), lambda i, *_: (i, 0)),
        scratch_shapes=[pltpu.VMEM((2, N_EXPERTS * sw, d), BF16), pltpu.SemaphoreType.DMA((2,)),
                        pltpu.VMEM((sw, d), BF16), pltpu.SemaphoreType.DMA(()),
                        pltpu.VMEM((tile, d), F32)],
    )
    return pl.pallas_call(
        functools.partial(_combine_kernel, cap=cap, tile=tile, sw=sw),
        grid_spec=grid_spec,
        out_shape=jax.ShapeDtypeStruct((n, d), F32),
        compiler_params=pltpu.CompilerParams(
            dimension_semantics=("arbitrary",), vmem_limit_bytes=VMEM_LIMIT),
        name="combine",
    )(win_start, win_n, x1, pos_tok, gate_tok, ye.reshape(N_EXPERTS * cap, d), p, w_ple, w_pg, b_pg, fg)


def _trunk(x, p, w):
    b, l, d = x.shape
    n = b * l
    cap = CAPACITY_FACTOR * n // N_EXPERTS
    rows = n // LANES

    x1, tok_rows, probs = _front(x, w["g1"], w["w_in"], w["conv_k"], w["conv_b"], w["ln_g"],
                                 w["ln_b"], w["pool_w"], w["pool_scale"], w["w_out"], w["g2"],
                                 w["w_rt"])

    chunk = min(SLOT_CHUNK, cap)
    n_chunks = cap // chunk
    pos3, rank3, off3, lo3, hi3 = _select(probs.reshape(N_EXPERTS, rows, LANES), cap, chunk)

    lo_row = lo3[:, 0, :n_chunks].reshape(-1)
    hi_row = hi3[:, 0, :n_chunks].reshape(-1)
    idx = _invert(rank3, lo_row, hi_row, n_chunks, chunk)
    ye = _experts(tok_rows, idx, w["w_gate"], w["w_up"], w["w_down"], cap)

    tile = min(COMBINE_TILE, n)
    sw = min(SLOT_WINDOW, cap)
    starts = off3[:, ::tile // LANES, 0]
    ends = jnp.concatenate([starts[:, 1:], jnp.full((N_EXPERTS, 1), cap, I32)], axis=1)
    win_start = jnp.minimum(starts // 16 * 16, cap - sw)
    win_n = jnp.maximum((ends - win_start + sw - 1) // sw, 1)
    y = _combine(x1.reshape(n, d), pos3.reshape(N_EXPERTS, n).T, probs.T, ye, p.reshape(n, PLE_DIM),
                 w["w_ple"], w["w_pg"], w["b_pg"], w["fg"],
                 win_start.T.reshape(-1), win_n.T.reshape(-1), cap)
    return y.reshape(b, l, d)


def kernel(x_prompt, x_sample, p_prompt, p_sample, norm1_g, w_in, conv_k, conv_b, conv_ln_g,
           conv_ln_b, pool_w, pool_scale, w_out, norm2_g, w_router, w_gate, w_up, w_down,
           w_ple, w_pg, b_pg, final_g):
    row = lambda v: v.reshape(1, -1).astype(F32)
    w = dict(
        g1=row(norm1_g[0]), w_in=w_in[0].astype(BF16), conv_k=conv_k[0].reshape(KERNEL_SIZE, CONV_W),
        conv_b=row(conv_b[0]), ln_g=row(conv_ln_g[0]), ln_b=row(conv_ln_b[0]),
        pool_w=pool_w[0].astype(BF16), pool_scale=row(pool_scale[0]), w_out=w_out[0].astype(BF16),
        g2=row(norm2_g[0]), w_rt=w_router[0].T.astype(BF16),
        w_gate=w_gate[0].astype(BF16), w_up=w_up[0].astype(BF16), w_down=w_down[0].astype(BF16),
        w_ple=w_ple[0].astype(BF16), w_pg=w_pg[0].astype(BF16), b_pg=row(b_pg[0]), fg=row(final_g),
    )
    return (_trunk(x_prompt, p_prompt[0], w), _trunk(x_sample, p_sample[0], w))
```

```python
import functools

import jax
import jax.numpy as jnp
from jax import lax
from jax.experimental import pallas as pl
from jax.experimental.pallas import tpu as pltpu

D_MODEL = 1024
CONV_W = 512
POOL_W = 512
POOL_GROUPS = 4
POOL_GC = 128
POOL_WINDOWS = (2, 4, 8, 16)
KERNEL_SIZE = 31
IN_COLS = 2 * CONV_W + POOL_W
N_EXPERTS = 16
CAPACITY_FACTOR = 2
D_EXPERT = 2 * D_MODEL
PLE_DIM = 256
EPS = 1e-6

LANES = 128
SUBLANES = 8
HALO = 16
VMEM_LIMIT = 56 * 1024 * 1024
ROW_SUB = D_MODEL // LANES

SEQ_TILE = 512
SLOT_CHUNK = 256
STEP_CHUNKS = 4
INVERT_GROUP = 4
ROW_WINDOW = LANES
COMBINE_TILE = 512
SLOT_WINDOW = 128

BF16 = jnp.bfloat16
F32 = jnp.float32
I32 = jnp.int32


def _dot(a, b):
    return jnp.dot(a, b, preferred_element_type=F32)


def _rms(x, g):
    return x * lax.rsqrt(jnp.mean(x * x, axis=-1, keepdims=True) + EPS) * g


def _front_kernel(xp_ref, xm_ref, xn_ref, g1_ref, win_ref, ck_ref, cb_ref, lg_ref, lb_ref,
                  pw_ref, ps_ref, wout_ref, g2_ref, wrt_ref,
                  x1_ref, row_ref, pr_ref, u_ref, zp_ref, *, seq_len, tile):
    t = pl.program_id(1)
    nt = pl.num_programs(1)
    rows = tile + 2 * HALO

    xs = jnp.concatenate([xp_ref[...], xm_ref[...], xn_ref[...]], axis=0)
    h = _rms(xs, g1_ref[...]).astype(BF16)
    z = _dot(h, win_ref[...])

    u_ref[0] = z[:, :CONV_W] * jax.nn.sigmoid(z[:, CONV_W:2 * CONV_W])
    zp_ref[...] = z[:, 2 * CONV_W:]
    for first, keep in ((0, t > 0), (HALO + tile, t < nt - 1)):
        edge = pl.ds(first, HALO)
        scale = jnp.where(keep, 1.0, 0.0).astype(F32)
        u_ref[0, edge, :] = u_ref[0, edge, :] * scale
        zp_ref[edge, :] = zp_ref[edge, :] * scale
    for r in range(1, SUBLANES):
        u_ref[r, pl.ds(0, rows - SUBLANES), :] = u_ref[0, pl.ds(r, rows - SUBLANES), :]

    conv = None
    for k in range(KERNEL_SIZE):
        off = HALO - KERNEL_SIZE // 2 + k
        r = off % SUBLANES
        term = u_ref[r, pl.ds(off - r, tile), :] * ck_ref[k:k + 1, :]
        conv = term if conv is None else conv + term
    conv = conv + cb_ref[...]
    mu = jnp.mean(conv, axis=-1, keepdims=True)
    cen = conv - mu
    var = jnp.mean(cen * cen, axis=-1, keepdims=True)
    yc = cen * lax.rsqrt(var + EPS) * lg_ref[...] + lb_ref[...]
    yc = yc * jax.nn.sigmoid(yc)

    pos = lax.broadcasted_iota(I32, (tile, LANES), 0) + t * tile
    yps = []
    for g, w in enumerate(POOL_WINDOWS):
        cols = pl.ds(g * POOL_GC, POOL_GC)
        s = None
        for j in range(-(w // 2), w // 2):
            term = zp_ref[pl.ds(HALO + j, tile), cols]
            s = term if s is None else s + term
        lo = jnp.maximum(pos - w // 2, 0)
        hi = jnp.minimum(pos + w // 2 - 1, seq_len - 1)
        cnt = (hi - lo + 1).astype(F32)
        pooled = s / cnt - zp_ref[pl.ds(HALO, tile), cols]
        yp = _dot(pooled.astype(BF16), pw_ref[g]) * ps_ref[:, g * POOL_GC:(g + 1) * POOL_GC]
        yps.append(yp)

    y = jnp.concatenate([yc] + yps, axis=1).astype(BF16)
    x1 = xm_ref[...] + _dot(y, wout_ref[...])
    x1_ref[...] = x1

    h2f = _rms(x1, g2_ref[...])
    h2 = h2f.astype(BF16)
    logits = lax.dot_general(wrt_ref[...], h2, (((1,), (1,)), ((), ())),
                             preferred_element_type=F32)
    m = jnp.max(logits, axis=0, keepdims=True)
    ex = jnp.exp(logits - m)
    pr_ref[...] = ex / jnp.sum(ex, axis=0, keepdims=True)
    for s in range(ROW_SUB):
        row_ref[pl.ds(s, tile, stride=ROW_SUB), :] = h2f[:, s * LANES:(s + 1) * LANES]


def _front(x, g1, w_in, conv_k, conv_b, ln_g, ln_b, pool_w, pool_scale, w_out, g2, w_rt):
    b, l, d = x.shape
    tile = min(SEQ_TILE, l)
    nt = l // tile
    hpt = tile // HALO
    nh = l // HALO
    n = b * l
    const = lambda shape: pl.BlockSpec(shape, lambda i, j: (0,) * len(shape))
    kern = functools.partial(_front_kernel, seq_len=l, tile=tile)
    return pl.pallas_call(
        kern,
        grid=(b, nt),
        in_specs=[
            pl.BlockSpec((None, HALO, d), lambda i, j: (i, jnp.maximum(j * hpt - 1, 0), 0)),
            pl.BlockSpec((None, tile, d), lambda i, j: (i, j, 0)),
            pl.BlockSpec((None, HALO, d), lambda i, j: (i, jnp.minimum((j + 1) * hpt, nh - 1), 0)),
            const((1, d)), const((d, IN_COLS)), const((KERNEL_SIZE, CONV_W)), const((1, CONV_W)),
            const((1, CONV_W)), const((1, CONV_W)), const((POOL_GROUPS, POOL_GC, POOL_GC)),
            const((1, POOL_W)), const((d, d)), const((1, d)), const((N_EXPERTS, d)),
        ],
        out_specs=[
            pl.BlockSpec((None, tile, d), lambda i, j: (i, j, 0)),
            pl.BlockSpec((tile * ROW_SUB, LANES), lambda i, j: (i * nt + j, 0)),
            pl.BlockSpec((N_EXPERTS, tile), lambda i, j: (0, i * nt + j)),
        ],
        out_shape=[
            jax.ShapeDtypeStruct((b, l, d), F32),
            jax.ShapeDtypeStruct((n * ROW_SUB, LANES), F32),
            jax.ShapeDtypeStruct((N_EXPERTS, n), F32),
        ],
        scratch_shapes=[pltpu.VMEM((SUBLANES, tile + 2 * HALO, CONV_W), F32),
                        pltpu.VMEM((tile + 2 * HALO, POOL_W), F32)],
        compiler_params=pltpu.CompilerParams(
            dimension_semantics=("arbitrary", "arbitrary"), vmem_limit_bytes=VMEM_LIMIT),
        name="front",
    )(x, x, x, g1, w_in, conv_k, conv_b, ln_g, ln_b, pool_w, pool_scale, w_out, g2, w_rt)


def _select_kernel(p_ref, pos_ref, rank_ref, off_ref, lo_ref, hi_ref, tot_ref, *, cap, chunk):
    e, r, _ = p_ref.shape
    p = p_ref[...]
    capf = jnp.float32(cap)

    def count_ge(thr):
        hit = jnp.where(p >= thr, 1.0, 0.0).astype(F32)
        return jnp.sum(jnp.sum(hit, axis=1, keepdims=True), axis=2, keepdims=True)

    def bisect(i, bits):
        cand = jnp.bitwise_or(bits, jnp.left_shift(jnp.int32(1), 30 - i))
        return jnp.where(count_ge(lax.bitcast_convert_type(cand, F32)) >= capf, cand, bits)

    tau_bits = lax.fori_loop(0, 31, bisect, jnp.zeros((e, 1, 1), I32))
    tau = lax.bitcast_convert_type(tau_bits, F32)
    above_tau = p >= lax.bitcast_convert_type(tau_bits + 1, F32)
    at_tau = jnp.logical_and(p >= tau, jnp.logical_not(above_tau))

    gt = jnp.where(above_tau, 1.0, 0.0).astype(BF16).reshape(e * r, LANES)
    eq = jnp.where(at_tau, 1.0, 0.0).astype(BF16).reshape(e * r, LANES)
    ri = lax.broadcasted_iota(I32, (LANES, LANES), 0)
    ci = lax.broadcasted_iota(I32, (LANES, LANES), 1)
    upper = jnp.where(ri <= ci, 1.0, 0.0).astype(BF16)
    ones = jnp.ones((LANES, LANES), BF16)
    rr = lax.broadcasted_iota(I32, (r, r), 0)
    rc = lax.broadcasted_iota(I32, (r, r), 1)
    lower = jnp.where(rc < rr, 1.0, 0.0).astype(BF16)

    def prefix(mask):
        within = _dot(mask, upper).reshape(e, r, LANES)
        tot_ref[...] = _dot(mask, ones).reshape(e, r, LANES)
        offs = [_dot(lower, tot_ref[i].astype(BF16)) for i in range(e)]
        return within, jnp.stack(offs, axis=0), tot_ref[...]

    gt_in, gt_off, gt_tot = prefix(gt)
    eq_in, eq_off, eq_tot = prefix(eq)
    n_gt = gt_off[:, r - 1:r, :] + gt_tot[:, r - 1:r, :]
    need = capf - n_gt
    eq_incl = eq_in + eq_off
    sel = jnp.logical_or(above_tau, jnp.logical_and(at_tau, eq_incl <= need))
    rank = gt_in + gt_off + jnp.minimum(eq_incl, need)
    rank_ref[...] = rank.astype(I32)
    pos_ref[...] = jnp.where(sel, rank - 1.0, -1.0).astype(I32)
    off_ref[...] = (gt_off + jnp.minimum(eq_off, need)).astype(I32)
    row_end = gt_off + gt_tot + jnp.minimum(eq_off + eq_tot, need)
    first = (lax.broadcasted_iota(I32, (1, 1, LANES), 2) * chunk).astype(F32)
    lo_ref[...] = jnp.sum(jnp.where(row_end <= first, 1.0, 0.0), axis=1, keepdims=True).astype(I32)
    hi_ref[...] = jnp.sum(jnp.where(row_end <= first + (chunk - 1.0), 1.0, 0.0),
                          axis=1, keepdims=True).astype(I32)


def _select(probs3, cap, chunk):
    e, r, _ = probs3.shape
    full = lambda s: pl.BlockSpec(s, lambda: (0,) * len(s))
    big = jax.ShapeDtypeStruct((e, r, LANES), I32)
    small = jax.ShapeDtypeStruct((e, 1, LANES), I32)
    return pl.pallas_call(
        functools.partial(_select_kernel, cap=cap, chunk=chunk),
        in_specs=[full((e, r, LANES))],
        out_specs=[full((e, r, LANES))] * 3 + [full((e, 1, LANES))] * 2,
        out_shape=[big, big, big, small, small],
        scratch_shapes=[pltpu.VMEM((e, r, LANES), F32)],
        compiler_params=pltpu.CompilerParams(vmem_limit_bytes=VMEM_LIMIT),
        name="select",
    )(probs3)


def _invert_kernel(lo_ref, hi_ref, rank_ref, off_ref, end_ref, idx_ref, acc_ref,
                   *, n_chunks, chunk, win):
    e = pl.program_id(0)
    local_row = lax.broadcasted_iota(I32, (win, LANES), 0).astype(F32).astype(BF16)
    ones = jnp.ones((win, LANES), BF16)
    srel = lax.broadcasted_iota(I32, (chunk, LANES), 0).astype(F32)

    def window(c, w):
        first = c * chunk
        row0 = pl.multiple_of(w * win, win)
        slot = lax.broadcasted_iota(I32, (chunk, win), 0) + first
        hit = jnp.logical_and(off_ref[:, pl.ds(row0, win)] <= slot, slot < end_ref[:, pl.ds(row0, win)])
        rel = jnp.clip(rank_ref[pl.ds(row0, win), :] - first, -1, chunk).astype(F32).astype(BF16)
        got = _dot(jnp.where(hit, 1.0, 0.0).astype(BF16),
                   jnp.concatenate([rel, local_row, ones], axis=1))
        cnt = jnp.sum(jnp.where(got[:, :LANES] <= srel, 1.0, 0.0), axis=1, keepdims=True)
        row = got[:, LANES:2 * LANES] + (w * win).astype(F32)
        return got[:, 2 * LANES:] * (row * LANES + cnt)

    def group(g, carry):
        base = g * INVERT_GROUP
        for j in range(INVERT_GROUP):
            c = base + j
            acc_ref[j] = window(c, lo_ref[e * n_chunks + c] // win)
        for j in range(INVERT_GROUP):
            c = base + j

            def more(w, carry2, j=j, c=c):
                acc_ref[j] += window(c, w)
                return carry2

            lax.fori_loop(lo_ref[e * n_chunks + c] // win + 1, hi_ref[e * n_chunks + c] // win + 1, more, 0)
        for j in range(INVERT_GROUP):
            idx_ref[pl.ds(base + j, 1), :] = acc_ref[j].T[0:1, :].astype(I32)
        return carry

    lax.fori_loop(0, n_chunks // INVERT_GROUP, group, 0)


def _invert(rank3, off_rows, end_rows, lo_row, hi_row, n_chunks, chunk):
    e, r, _ = rank3.shape
    win = min(ROW_WINDOW, r)
    grid_spec = pltpu.PrefetchScalarGridSpec(
        num_scalar_prefetch=2,
        grid=(e,),
        in_specs=[pl.BlockSpec((None, r, LANES), lambda i, *_: (i, 0, 0)),
                  pl.BlockSpec((None, 1, r), lambda i, *_: (i, 0, 0)),
                  pl.BlockSpec((None, 1, r), lambda i, *_: (i, 0, 0))],
        out_specs=pl.BlockSpec((None, n_chunks, chunk), lambda i, *_: (i, 0, 0)),
        scratch_shapes=[pltpu.VMEM((INVERT_GROUP, chunk, LANES), F32)],
    )
    return pl.pallas_call(
        functools.partial(_invert_kernel, n_chunks=n_chunks, chunk=chunk, win=win),
        grid_spec=grid_spec,
        out_shape=jax.ShapeDtypeStruct((e, n_chunks, chunk), I32),
        compiler_params=pltpu.CompilerParams(dimension_semantics=("arbitrary",)),
        name="invert",
    )(lo_row, hi_row, rank3, off_rows.reshape(e, 1, r), end_rows.reshape(e, 1, r))


def _experts_kernel(cur_ref, nxt_ref, row_hbm, wg_ref, wu_ref, wd_ref, ye_ref, buf, sem,
                    xe_ref, *, chunk):
    step = pl.program_id(0) * pl.num_programs(1) + pl.program_id(1)
    n_steps = pl.num_programs(0) * pl.num_programs(1)
    slot = lax.rem(step, 2)

    def gather(idx_ref, s, h):
        for r in range(chunk):
            src = pl.multiple_of(idx_ref[0, h * chunk + r] * ROW_SUB, ROW_SUB)
            pltpu.make_async_copy(row_hbm.at[pl.ds(src, ROW_SUB), :],
                                  buf.at[s, h, pl.ds(r * ROW_SUB, ROW_SUB), :], sem.at[s, h]).start()

    def wait(s, h):
        pltpu.make_async_copy(row_hbm.at[pl.ds(0, chunk * ROW_SUB), :], buf.at[s, h], sem.at[s, h]).wait()

    @pl.when(step == 0)
    def _():
        for h in range(STEP_CHUNKS):
            gather(cur_ref, 0, h)

    for h in range(STEP_CHUNKS):
        wait(slot, h)
    for h in range(STEP_CHUNKS):
        gather(nxt_ref, 1 - slot, h)
        for k in range(ROW_SUB):
            xe_ref[h, :, k * LANES:(k + 1) * LANES] = (
                buf[slot, h, pl.ds(k, chunk, stride=ROW_SUB), :].astype(BF16))
        xe = xe_ref[h]
        a = _dot(xe, wg_ref[...])
        b = _dot(xe, wu_ref[...])
        hid = (a * jax.nn.sigmoid(a) * b).astype(BF16)
        ye_ref[h * chunk:(h + 1) * chunk, :] = _dot(hid, wd_ref[...]).astype(BF16)

    @pl.when(step == n_steps - 1)
    def _():
        for h in range(STEP_CHUNKS):
            wait(1 - slot, h)


def _experts(rows, idx, wg, wu, wd, cap):
    d = D_MODEL
    e, n_chunks, chunk = idx.shape
    per_step = STEP_CHUNKS * chunk
    n_inner = n_chunks // STEP_CHUNKS
    n_steps = e * n_inner
    idx = idx.reshape(n_steps, 1, per_step)
    smem = lambda f: pl.BlockSpec((None, 1, per_step), f, memory_space=pltpu.SMEM)
    return pl.pallas_call(
        functools.partial(_experts_kernel, chunk=chunk),
        grid=(e, n_inner),
        in_specs=[
            smem(lambda i, c: (i * n_inner + c, 0, 0)),
            smem(lambda i, c: (jnp.minimum(i * n_inner + c + 1, n_steps - 1), 0, 0)),
            pl.BlockSpec(memory_space=pl.ANY),
            pl.BlockSpec((None, d, D_EXPERT), lambda i, c: (i, 0, 0)),
            pl.BlockSpec((None, d, D_EXPERT), lambda i, c: (i, 0, 0)),
            pl.BlockSpec((None, D_EXPERT, d), lambda i, c: (i, 0, 0)),
        ],
        out_specs=pl.BlockSpec((None, per_step, d), lambda i, c: (i, c, 0)),
        out_shape=jax.ShapeDtypeStruct((e, cap, d), BF16),
        scratch_shapes=[pltpu.VMEM((2, STEP_CHUNKS, chunk * ROW_SUB, LANES), F32),
                        pltpu.SemaphoreType.DMA((2, STEP_CHUNKS)),
                        pltpu.VMEM((STEP_CHUNKS, chunk, D_MODEL), BF16)],
        compiler_params=pltpu.CompilerParams(
            dimension_semantics=("arbitrary", "arbitrary"), vmem_limit_bytes=VMEM_LIMIT),
        name="experts",
    )(idx, idx, rows, wg, wu, wd)


def _combine_kernel(ws_ref, wn_ref, x1_ref, pos_ref, pr_ref, ye_hbm, p_ref, wple_ref, wpg_ref, bpg_ref,
                    fg_ref, y_ref, ybuf, sem, xbuf, xsem, acc_ref, *, cap, tile, sw):
    i = pl.program_id(0)
    n = pl.num_programs(0)
    slot = lax.rem(i, 2)

    def copies(step, s):
        return [pltpu.make_async_copy(
            ye_hbm.at[pl.ds(pl.multiple_of(e * cap + ws_ref[step * N_EXPERTS + e], 16), sw), :],
            ybuf.at[s, pl.ds(e * sw, sw), :], sem.at[s]) for e in range(N_EXPERTS)]

    @pl.when(i == 0)
    def _():
        for cp in copies(0, 0):
            cp.start()

    @pl.when(i + 1 < n)
    def _():
        for cp in copies(i + 1, 1 - slot):
            cp.start()

    for cp in copies(i, slot):
        cp.wait()

    pos = pos_ref[...]
    gates = pr_ref[...]
    lane = lax.broadcasted_iota(I32, (tile, sw), 1)
    hits = []
    for e in range(N_EXPERTS):
        rel = pos[:, e:e + 1] - ws_ref[i * N_EXPERTS + e]
        hits.append(jnp.where(rel == lane, gates[:, e:e + 1], 0.0).astype(BF16))
    acc_ref[...] = _dot(jnp.concatenate(hits, axis=1), ybuf[slot])

    for e in range(N_EXPERTS):
        first = ws_ref[i * N_EXPERTS + e]

        def extra(w, carry, e=e, first=first):
            want = first + w * sw
            start = pl.multiple_of(jnp.minimum(want, cap - sw), 16)
            cp = pltpu.make_async_copy(ye_hbm.at[pl.ds(e * cap + start, sw), :], xbuf, xsem)
            cp.start()
            cp.wait()
            col = pos[:, e:e + 1]
            hit = jnp.logical_and(col - start == lane, col >= want)
            acc_ref[...] += _dot(jnp.where(hit, gates[:, e:e + 1], 0.0).astype(BF16), xbuf[...])
            return carry

        lax.fori_loop(1, wn_ref[i * N_EXPERTS + e], extra, 0)

    x2 = x1_ref[...] + acc_ref[...]
    gate = jax.nn.sigmoid(_dot(x2.astype(BF16), wpg_ref[...]) + bpg_ref[...])
    x3 = x2 + gate * _dot(p_ref[...].astype(BF16), wple_ref[...])
    y_ref[...] = _rms(x3, fg_ref[...])


def _combine(x1, pos_tok, gate_tok, ye, p, w_ple, w_pg, b_pg, fg, win_start, win_n, cap):
    n, d = x1.shape
    tile = min(COMBINE_TILE, n)
    sw = min(SLOT_WINDOW, cap)
    const = lambda shape: pl.BlockSpec(shape, lambda i, *_: (0,) * len(shape))
    grid_spec = pltpu.PrefetchScalarGridSpec(
        num_scalar_prefetch=2,
        grid=(n // tile,),
        in_specs=[
            pl.BlockSpec((tile, d), lambda i, *_: (i, 0)),
            pl.BlockSpec((tile, N_EXPERTS), lambda i, *_: (i, 0)),
            pl.BlockSpec((tile, N_EXPERTS), lambda i, *_: (i, 0)),
            pl.BlockSpec(memory_space=pl.ANY),
            pl.BlockSpec((tile, PLE_DIM), lambda i, *_: (i, 0)),
            const((PLE_DIM, d)), const((d, d)), const((1, d)), const((1, d)),
        ],
        out_specs=pl.BlockSpec((tile, d), lambda i, *_: (i, 0)),
        scratch_shapes=[pltpu.VMEM((2, N_EXPERTS * sw, d), BF16), pltpu.SemaphoreType.DMA((2,)),
                        pltpu.VMEM((sw, d), BF16), pltpu.SemaphoreType.DMA(()),
                        pltpu.VMEM((tile, d), F32)],
    )
    return pl.pallas_call(
        functools.partial(_combine_kernel, cap=cap, tile=tile, sw=sw),
        grid_spec=grid_spec,
        out_shape=jax.ShapeDtypeStruct((n, d), F32),
        compiler_params=pltpu.CompilerParams(
            dimension_semantics=("arbitrary",), vmem_limit_bytes=VMEM_LIMIT),
        name="combine",
    )(win_start, win_n, x1, pos_tok, gate_tok, ye.reshape(N_EXPERTS * cap, d), p, w_ple, w_pg, b_pg, fg)


def _trunk(x, p, w):
    b, l, d = x.shape
    n = b * l
    cap = CAPACITY_FACTOR * n // N_EXPERTS
    rows = n // LANES

    x1, tok_rows, probs = _front(x, w["g1"], w["w_in"], w["conv_k"], w["conv_b"], w["ln_g"],
                                 w["ln_b"], w["pool_w"], w["pool_scale"], w["w_out"], w["g2"],
                                 w["w_rt"])

    chunk = min(SLOT_CHUNK, cap)
    n_chunks = cap // chunk
    pos3, rank3, off3, lo3, hi3 = _select(probs.reshape(N_EXPERTS, rows, LANES), cap, chunk)

    lo_row = lo3[:, 0, :n_chunks].reshape(-1)
    hi_row = hi3[:, 0, :n_chunks].reshape(-1)
    off_rows = off3[:, :, 0]
    end_rows = jnp.concatenate([off_rows[:, 1:], jnp.full((N_EXPERTS, 1), cap, I32)], axis=1)
    idx = _invert(rank3, off_rows, end_rows, lo_row, hi_row, n_chunks, chunk)
    ye = _experts(tok_rows, idx, w["w_gate"], w["w_up"], w["w_down"], cap)

    tile = min(COMBINE_TILE, n)
    sw = min(SLOT_WINDOW, cap)
    starts = off3[:, ::tile // LANES, 0]
    ends = jnp.concatenate([starts[:, 1:], jnp.full((N_EXPERTS, 1), cap, I32)], axis=1)
    win_start = jnp.minimum(starts // 16 * 16, cap - sw)
    win_n = jnp.maximum((ends - win_start + sw - 1) // sw, 1)
    y = _combine(x1.reshape(n, d), pos3.reshape(N_EXPERTS, n).T, probs.T, ye, p.reshape(n, PLE_DIM),
                 w["w_ple"], w["w_pg"], w["b_pg"], w["fg"],
                 win_start.T.reshape(-1), win_n.T.reshape(-1), cap)
    return y.reshape(b, l, d)


def kernel(x_prompt, x_sample, p_prompt, p_sample, norm1_g, w_in, conv_k, conv_b, conv_ln_g,
           conv_ln_b, pool_w, pool_scale, w_out, norm2_g, w_router, w_gate, w_up, w_down,
           w_ple, w_pg, b_pg, final_g):
    row = lambda v: v.reshape(1, -1).astype(F32)
    w = dict(
        g1=row(norm1_g[0]), w_in=w_in[0].astype(BF16), conv_k=conv_k[0].reshape(KERNEL_SIZE, CONV_W),
        conv_b=row(conv_b[0]), ln_g=row(conv_ln_g[0]), ln_b=row(conv_ln_b[0]),
        pool_w=pool_w[0].astype(BF16), pool_scale=row(pool_scale[0]), w_out=w_out[0].astype(BF16),
        g2=row(norm2_g[0]), w_rt=w_router[0].T.astype(BF16),
        w_gate=w_gate[0].astype(BF16), w_up=w_up[0].astype(BF16), w_down=w_down[0].astype(BF16),
        w_ple=w_ple[0].astype(BF16), w_pg=w_pg[0].astype(BF16), b_pg=row(b_pg[0]), fg=row(final_g),
    )
    return (_trunk(x_prompt, p_prompt[0], w), _trunk(x_sample, p_sample[0], w))
```

```python
import functools

import jax
import jax.numpy as jnp
from jax import lax
from jax.experimental import pallas as pl
from jax.experimental.pallas import tpu as pltpu

D_MODEL = 1024
CONV_W = 512
POOL_W = 512
POOL_GROUPS = 4
POOL_GC = 128
POOL_WINDOWS = (2, 4, 8, 16)
KERNEL_SIZE = 31
IN_COLS = 2 * CONV_W + POOL_W
N_EXPERTS = 16
CAPACITY_FACTOR = 2
D_EXPERT = 2 * D_MODEL
PLE_DIM = 256
EPS = 1e-6

LANES = 128
SUBLANES = 8
HALO = 16
VMEM_LIMIT = 56 * 1024 * 1024
ROW_SUB = D_MODEL // LANES

SEQ_TILE = 512
SLOT_CHUNK = 256
STEP_CHUNKS = 4
INVERT_GROUP = 4
ROW_WINDOW = LANES
COMBINE_TILE = 512
COMBINE_PARTS = 2
SLOT_WINDOW = 128

BF16 = jnp.bfloat16
F32 = jnp.float32
I32 = jnp.int32


def _dot(a, b):
    return jnp.dot(a, b, preferred_element_type=F32)


def _rms(x, g):
    return x * lax.rsqrt(jnp.mean(x * x, axis=-1, keepdims=True) + EPS) * g


def _front_kernel(xp_ref, xm_ref, xn_ref, g1_ref, win_ref, ck_ref, cb_ref, lg_ref, lb_ref,
                  pw_ref, ps_ref, wout_ref, g2_ref, wrt_ref,
                  x1_ref, row_ref, pr_ref, u_ref, zp_ref, *, seq_len, tile):
    t = pl.program_id(1)
    nt = pl.num_programs(1)
    rows = tile + 2 * HALO

    xs = jnp.concatenate([xp_ref[...], xm_ref[...], xn_ref[...]], axis=0)
    h = _rms(xs, g1_ref[...]).astype(BF16)
    z = _dot(h, win_ref[...])

    u_ref[0] = z[:, :CONV_W] * jax.nn.sigmoid(z[:, CONV_W:2 * CONV_W])
    zp_ref[...] = z[:, 2 * CONV_W:]
    for first, keep in ((0, t > 0), (HALO + tile, t < nt - 1)):
        edge = pl.ds(first, HALO)
        scale = jnp.where(keep, 1.0, 0.0).astype(F32)
        u_ref[0, edge, :] = u_ref[0, edge, :] * scale
        zp_ref[edge, :] = zp_ref[edge, :] * scale
    for r in range(1, SUBLANES):
        u_ref[r, pl.ds(0, rows - SUBLANES), :] = u_ref[0, pl.ds(r, rows - SUBLANES), :]

    conv = None
    for k in range(KERNEL_SIZE):
        off = HALO - KERNEL_SIZE // 2 + k
        r = off % SUBLANES
        term = u_ref[r, pl.ds(off - r, tile), :] * ck_ref[k:k + 1, :]
        conv = term if conv is None else conv + term
    conv = conv + cb_ref[...]
    mu = jnp.mean(conv, axis=-1, keepdims=True)
    cen = conv - mu
    var = jnp.mean(cen * cen, axis=-1, keepdims=True)
    yc = cen * lax.rsqrt(var + EPS) * lg_ref[...] + lb_ref[...]
    yc = yc * jax.nn.sigmoid(yc)

    pos = lax.broadcasted_iota(I32, (tile, LANES), 0) + t * tile
    yps = []
    for g, w in enumerate(POOL_WINDOWS):
        cols = pl.ds(g * POOL_GC, POOL_GC)
        s = None
        for j in range(-(w // 2), w // 2):
            term = zp_ref[pl.ds(HALO + j, tile), cols]
            s = term if s is None else s + term
        lo = jnp.maximum(pos - w // 2, 0)
        hi = jnp.minimum(pos + w // 2 - 1, seq_len - 1)
        cnt = (hi - lo + 1).astype(F32)
        pooled = s / cnt - zp_ref[pl.ds(HALO, tile), cols]
        yp = _dot(pooled.astype(BF16), pw_ref[g]) * ps_ref[:, g * POOL_GC:(g + 1) * POOL_GC]
        yps.append(yp)

    y = jnp.concatenate([yc] + yps, axis=1).astype(BF16)
    x1 = xm_ref[...] + _dot(y, wout_ref[...])
    x1_ref[...] = x1

    h2f = _rms(x1, g2_ref[...])
    h2 = h2f.astype(BF16)
    logits = lax.dot_general(wrt_ref[...], h2, (((1,), (1,)), ((), ())),
                             preferred_element_type=F32)
    m = jnp.max(logits, axis=0, keepdims=True)
    ex = jnp.exp(logits - m)
    pr_ref[...] = ex / jnp.sum(ex, axis=0, keepdims=True)
    for s in range(ROW_SUB):
        row_ref[pl.ds(s, tile, stride=ROW_SUB), :] = h2f[:, s * LANES:(s + 1) * LANES]


def _front(x, g1, w_in, conv_k, conv_b, ln_g, ln_b, pool_w, pool_scale, w_out, g2, w_rt):
    b, l, d = x.shape
    tile = min(SEQ_TILE, l)
    nt = l // tile
    hpt = tile // HALO
    nh = l // HALO
    n = b * l
    const = lambda shape: pl.BlockSpec(shape, lambda i, j: (0,) * len(shape))
    kern = functools.partial(_front_kernel, seq_len=l, tile=tile)
    return pl.pallas_call(
        kern,
        grid=(b, nt),
        in_specs=[
            pl.BlockSpec((None, HALO, d), lambda i, j: (i, jnp.maximum(j * hpt - 1, 0), 0)),
            pl.BlockSpec((None, tile, d), lambda i, j: (i, j, 0)),
            pl.BlockSpec((None, HALO, d), lambda i, j: (i, jnp.minimum((j + 1) * hpt, nh - 1), 0)),
            const((1, d)), const((d, IN_COLS)), const((KERNEL_SIZE, CONV_W)), const((1, CONV_W)),
            const((1, CONV_W)), const((1, CONV_W)), const((POOL_GROUPS, POOL_GC, POOL_GC)),
            const((1, POOL_W)), const((d, d)), const((1, d)), const((N_EXPERTS, d)),
        ],
        out_specs=[
            pl.BlockSpec((None, tile, d), lambda i, j: (i, j, 0)),
            pl.BlockSpec((tile * ROW_SUB, LANES), lambda i, j: (i * nt + j, 0)),
            pl.BlockSpec((N_EXPERTS, tile), lambda i, j: (0, i * nt + j)),
        ],
        out_shape=[
            jax.ShapeDtypeStruct((b, l, d), F32),
            jax.ShapeDtypeStruct((n * ROW_SUB, LANES), F32),
            jax.ShapeDtypeStruct((N_EXPERTS, n), F32),
        ],
        scratch_shapes=[pltpu.VMEM((SUBLANES, tile + 2 * HALO, CONV_W), F32),
                        pltpu.VMEM((tile + 2 * HALO, POOL_W), F32)],
        compiler_params=pltpu.CompilerParams(
            dimension_semantics=("arbitrary", "arbitrary"), vmem_limit_bytes=VMEM_LIMIT),
        name="front",
    )(x, x, x, g1, w_in, conv_k, conv_b, ln_g, ln_b, pool_w, pool_scale, w_out, g2, w_rt)


def _select_kernel(p_ref, pos_ref, rank_ref, off_ref, lo_ref, hi_ref, tot_ref, *, cap, chunk):
    e, r, _ = p_ref.shape
    p = p_ref[...]
    capf = jnp.float32(cap)

    def count_ge(thr):
        hit = jnp.where(p >= thr, 1.0, 0.0).astype(F32)
        return jnp.sum(jnp.sum(hit, axis=1, keepdims=True), axis=2, keepdims=True)

    def bisect(i, bits):
        cand = jnp.bitwise_or(bits, jnp.left_shift(jnp.int32(1), 30 - i))
        return jnp.where(count_ge(lax.bitcast_convert_type(cand, F32)) >= capf, cand, bits)

    tau_bits = lax.fori_loop(0, 31, bisect, jnp.zeros((e, 1, 1), I32))
    tau = lax.bitcast_convert_type(tau_bits, F32)
    above_tau = p >= lax.bitcast_convert_type(tau_bits + 1, F32)
    at_tau = jnp.logical_and(p >= tau, jnp.logical_not(above_tau))

    gt = jnp.where(above_tau, 1.0, 0.0).astype(BF16).reshape(e * r, LANES)
    eq = jnp.where(at_tau, 1.0, 0.0).astype(BF16).reshape(e * r, LANES)
    ri = lax.broadcasted_iota(I32, (LANES, LANES), 0)
    ci = lax.broadcasted_iota(I32, (LANES, LANES), 1)
    upper = jnp.where(ri <= ci, 1.0, 0.0).astype(BF16)
    ones = jnp.ones((LANES, LANES), BF16)
    rr = lax.broadcasted_iota(I32, (r, r), 0)
    rc = lax.broadcasted_iota(I32, (r, r), 1)
    lower = jnp.where(rc < rr, 1.0, 0.0).astype(BF16)

    def prefix(mask):
        within = _dot(mask, upper).reshape(e, r, LANES)
        tot_ref[...] = _dot(mask, ones).reshape(e, r, LANES)
        offs = [_dot(lower, tot_ref[i].astype(BF16)) for i in range(e)]
        return within, jnp.stack(offs, axis=0), tot_ref[...]

    gt_in, gt_off, gt_tot = prefix(gt)
    eq_in, eq_off, eq_tot = prefix(eq)
    n_gt = gt_off[:, r - 1:r, :] + gt_tot[:, r - 1:r, :]
    need = capf - n_gt
    eq_incl = eq_in + eq_off
    sel = jnp.logical_or(above_tau, jnp.logical_and(at_tau, eq_incl <= need))
    rank = gt_in + gt_off + jnp.minimum(eq_incl, need)
    rank_ref[...] = rank.astype(I32)
    pos_ref[...] = jnp.where(sel, rank - 1.0, -1.0).astype(I32)
    off_ref[...] = (gt_off + jnp.minimum(eq_off, need)).astype(I32)
    row_end = gt_off + gt_tot + jnp.minimum(eq_off + eq_tot, need)
    first = (lax.broadcasted_iota(I32, (1, 1, LANES), 2) * chunk).astype(F32)
    lo_ref[...] = jnp.sum(jnp.where(row_end <= first, 1.0, 0.0), axis=1, keepdims=True).astype(I32)
    hi_ref[...] = jnp.sum(jnp.where(row_end <= first + (chunk - 1.0), 1.0, 0.0),
                          axis=1, keepdims=True).astype(I32)


def _select(probs3, cap, chunk):
    e, r, _ = probs3.shape
    full = lambda s: pl.BlockSpec(s, lambda: (0,) * len(s))
    big = jax.ShapeDtypeStruct((e, r, LANES), I32)
    small = jax.ShapeDtypeStruct((e, 1, LANES), I32)
    return pl.pallas_call(
        functools.partial(_select_kernel, cap=cap, chunk=chunk),
        in_specs=[full((e, r, LANES))],
        out_specs=[full((e, r, LANES))] * 3 + [full((e, 1, LANES))] * 2,
        out_shape=[big, big, big, small, small],
        scratch_shapes=[pltpu.VMEM((e, r, LANES), F32)],
        compiler_params=pltpu.CompilerParams(vmem_limit_bytes=VMEM_LIMIT),
        name="select",
    )(probs3)


def _invert_kernel(lo_ref, hi_ref, rank_ref, off_ref, end_ref, idx_ref, acc_ref,
                   *, n_chunks, chunk, win):
    e = pl.program_id(0)
    local_row = lax.broadcasted_iota(I32, (win, LANES), 0).astype(F32).astype(BF16)
    ones = jnp.ones((win, LANES), BF16)
    srel = lax.broadcasted_iota(I32, (chunk, LANES), 0).astype(F32)

    def window(c, w):
        first = c * chunk
        row0 = pl.multiple_of(w * win, win)
        slot = lax.broadcasted_iota(I32, (chunk, win), 0) + first
        hit = jnp.logical_and(off_ref[:, pl.ds(row0, win)] <= slot, slot < end_ref[:, pl.ds(row0, win)])
        rel = jnp.clip(rank_ref[pl.ds(row0, win), :] - first, -1, chunk).astype(F32).astype(BF16)
        got = _dot(jnp.where(hit, 1.0, 0.0).astype(BF16),
                   jnp.concatenate([rel, local_row, ones], axis=1))
        cnt = jnp.sum(jnp.where(got[:, :LANES] <= srel, 1.0, 0.0), axis=1, keepdims=True)
        row = got[:, LANES:2 * LANES] + (w * win).astype(F32)
        return got[:, 2 * LANES:] * (row * LANES + cnt)

    def group(g, carry):
        base = g * INVERT_GROUP
        for j in range(INVERT_GROUP):
            c = base + j
            acc_ref[j] = window(c, lo_ref[e * n_chunks + c] // win)
        for j in range(INVERT_GROUP):
            c = base + j

            def more(w, carry2, j=j, c=c):
                acc_ref[j] += window(c, w)
                return carry2

            lax.fori_loop(lo_ref[e * n_chunks + c] // win + 1, hi_ref[e * n_chunks + c] // win + 1, more, 0)
        for j in range(INVERT_GROUP):
            idx_ref[pl.ds(base + j, 1), :] = acc_ref[j].T[0:1, :].astype(I32)
        return carry

    lax.fori_loop(0, n_chunks // INVERT_GROUP, group, 0)


def _invert(rank3, off_rows, end_rows, lo_row, hi_row, n_chunks, chunk):
    e, r, _ = rank3.shape
    win = min(ROW_WINDOW, r)
    grid_spec = pltpu.PrefetchScalarGridSpec(
        num_scalar_prefetch=2,
        grid=(e,),
        in_specs=[pl.BlockSpec((None, r, LANES), lambda i, *_: (i, 0, 0)),
                  pl.BlockSpec((None, 1, r), lambda i, *_: (i, 0, 0)),
                  pl.BlockSpec((None, 1, r), lambda i, *_: (i, 0, 0))],
        out_specs=pl.BlockSpec((None, n_chunks, chunk), lambda i, *_: (i, 0, 0)),
        scratch_shapes=[pltpu.VMEM((INVERT_GROUP, chunk, LANES), F32)],
    )
    return pl.pallas_call(
        functools.partial(_invert_kernel, n_chunks=n_chunks, chunk=chunk, win=win),
        grid_spec=grid_spec,
        out_shape=jax.ShapeDtypeStruct((e, n_chunks, chunk), I32),
        compiler_params=pltpu.CompilerParams(dimension_semantics=("arbitrary",)),
        name="invert",
    )(lo_row, hi_row, rank3, off_rows.reshape(e, 1, r), end_rows.reshape(e, 1, r))


def _experts_kernel(cur_ref, nxt_ref, row_hbm, wg_ref, wu_ref, wd_ref, ye_ref, buf, sem,
                    xe_ref, *, chunk):
    step = pl.program_id(0) * pl.num_programs(1) + pl.program_id(1)
    n_steps = pl.num_programs(0) * pl.num_programs(1)
    slot = lax.rem(step, 2)

    def gather(idx_ref, s, h):
        for r in range(chunk):
            src = pl.multiple_of(idx_ref[0, h * chunk + r] * ROW_SUB, ROW_SUB)
            pltpu.make_async_copy(row_hbm.at[pl.ds(src, ROW_SUB), :],
                                  buf.at[s, h, pl.ds(r * ROW_SUB, ROW_SUB), :], sem.at[s, h]).start()

    def wait(s, h):
        pltpu.make_async_copy(row_hbm.at[pl.ds(0, chunk * ROW_SUB), :], buf.at[s, h], sem.at[s, h]).wait()

    @pl.when(step == 0)
    def _():
        for h in range(STEP_CHUNKS):
            gather(cur_ref, 0, h)

    for h in range(STEP_CHUNKS):
        wait(slot, h)
    for h in range(STEP_CHUNKS):
        gather(nxt_ref, 1 - slot, h)
        for k in range(ROW_SUB):
            xe_ref[h, :, k * LANES:(k + 1) * LANES] = (
                buf[slot, h, pl.ds(k, chunk, stride=ROW_SUB), :].astype(BF16))
        xe = xe_ref[h]
        a = _dot(xe, wg_ref[...])
        b = _dot(xe, wu_ref[...])
        hid = (a * jax.nn.sigmoid(a) * b).astype(BF16)
        ye_ref[h * chunk:(h + 1) * chunk, :] = _dot(hid, wd_ref[...]).astype(BF16)

    @pl.when(step == n_steps - 1)
    def _():
        for h in range(STEP_CHUNKS):
            wait(1 - slot, h)


def _experts(rows, idx, wg, wu, wd, cap):
    d = D_MODEL
    e, n_chunks, chunk = idx.shape
    per_step = STEP_CHUNKS * chunk
    n_inner = n_chunks // STEP_CHUNKS
    n_steps = e * n_inner
    idx = idx.reshape(n_steps, 1, per_step)
    smem = lambda f: pl.BlockSpec((None, 1, per_step), f, memory_space=pltpu.SMEM)
    return pl.pallas_call(
        functools.partial(_experts_kernel, chunk=chunk),
        grid=(e, n_inner),
        in_specs=[
            smem(lambda i, c: (i * n_inner + c, 0, 0)),
            smem(lambda i, c: (jnp.minimum(i * n_inner + c + 1, n_steps - 1), 0, 0)),
            pl.BlockSpec(memory_space=pl.ANY),
            pl.BlockSpec((None, d, D_EXPERT), lambda i, c: (i, 0, 0)),
            pl.BlockSpec((None, d, D_EXPERT), lambda i, c: (i, 0, 0)),
            pl.BlockSpec((None, D_EXPERT, d), lambda i, c: (i, 0, 0)),
        ],
        out_specs=pl.BlockSpec((None, per_step, d), lambda i, c: (i, c, 0)),
        out_shape=jax.ShapeDtypeStruct((e, cap, d), BF16),
        scratch_shapes=[pltpu.VMEM((2, STEP_CHUNKS, chunk * ROW_SUB, LANES), F32),
                        pltpu.SemaphoreType.DMA((2, STEP_CHUNKS)),
                        pltpu.VMEM((STEP_CHUNKS, chunk, D_MODEL), BF16)],
        compiler_params=pltpu.CompilerParams(
            dimension_semantics=("arbitrary", "arbitrary"), vmem_limit_bytes=VMEM_LIMIT),
        name="experts",
    )(idx, idx, rows, wg, wu, wd)


def _combine_kernel(ws_ref, wn_ref, x1_ref, pos_ref, pr_ref, ye_hbm, p_ref, wple_ref, wpg_ref, bpg_ref,
                    fg_ref, y_ref, ybuf, sem, xbuf, xsem, acc_ref, *, cap, tile, sw):
    i = pl.program_id(0)
    n = pl.num_programs(0)
    slot = lax.rem(i, 2)

    def copies(step, s):
        return [pltpu.make_async_copy(
            ye_hbm.at[pl.ds(pl.multiple_of(
                e * cap + ws_ref[(step * COMBINE_PARTS + q) * N_EXPERTS + e], 16), sw), :],
            ybuf.at[s, q, pl.ds(e * sw, sw), :], sem.at[s])
            for q in range(COMBINE_PARTS) for e in range(N_EXPERTS)]

    @pl.when(i == 0)
    def _():
        for cp in copies(0, 0):
            cp.start()

    @pl.when(i + 1 < n)
    def _():
        for cp in copies(i + 1, 1 - slot):
            cp.start()

    for cp in copies(i, slot):
        cp.wait()

    lane = lax.broadcasted_iota(I32, (tile, sw), 1)
    for q in range(COMBINE_PARTS):
        part = (i * COMBINE_PARTS + q) * N_EXPERTS
        pos = pos_ref[q * tile:(q + 1) * tile, :]
        gates = pr_ref[q * tile:(q + 1) * tile, :]
        hits = []
        for e in range(N_EXPERTS):
            rel = pos[:, e:e + 1] - ws_ref[part + e]
            hits.append(jnp.where(rel == lane, gates[:, e:e + 1], 0.0).astype(BF16))
        acc_ref[q] = _dot(jnp.concatenate(hits, axis=1), ybuf[slot, q])

    for q in range(COMBINE_PARTS):
        part = (i * COMBINE_PARTS + q) * N_EXPERTS
        for e in range(N_EXPERTS):
            first = ws_ref[part + e]

            def extra(w, carry, q=q, e=e, first=first):
                want = first + w * sw
                start = pl.multiple_of(jnp.minimum(want, cap - sw), 16)
                cp = pltpu.make_async_copy(ye_hbm.at[pl.ds(e * cap + start, sw), :], xbuf, xsem)
                cp.start()
                cp.wait()
                col = pos_ref[q * tile:(q + 1) * tile, e:e + 1]
                hit = jnp.logical_and(col - start == lane, col >= want)
                weight = jnp.where(hit, pr_ref[q * tile:(q + 1) * tile, e:e + 1], 0.0)
                acc_ref[q] += _dot(weight.astype(BF16), xbuf[...])
                return carry

            lax.fori_loop(1, wn_ref[part + e], extra, 0)

    for q in range(COMBINE_PARTS):
        rows = slice(q * tile, (q + 1) * tile)
        x2 = x1_ref[rows, :] + acc_ref[q]
        gate = jax.nn.sigmoid(_dot(x2.astype(BF16), wpg_ref[...]) + bpg_ref[...])
        x3 = x2 + gate * _dot(p_ref[rows, :].astype(BF16), wple_ref[...])
        y_ref[rows, :] = _rms(x3, fg_ref[...])


def _combine(x1, pos_tok, gate_tok, ye, p, w_ple, w_pg, b_pg, fg, win_start, win_n, cap):
    n, d = x1.shape
    tile = min(COMBINE_TILE, n // COMBINE_PARTS)
    step = tile * COMBINE_PARTS
    sw = min(SLOT_WINDOW, cap)
    const = lambda shape: pl.BlockSpec(shape, lambda i, *_: (0,) * len(shape))
    grid_spec = pltpu.PrefetchScalarGridSpec(
        num_scalar_prefetch=2,
        grid=(n // step,),
        in_specs=[
            pl.BlockSpec((step, d), lambda i, *_: (i, 0)),
            pl.BlockSpec((step, N_EXPERTS), lambda i, *_: (i, 0)),
            pl.BlockSpec((step, N_EXPERTS), lambda i, *_: (i, 0)),
            pl.BlockSpec(memory_space=pl.ANY),
            pl.BlockSpec((step, PLE_DIM), lambda i, *_: (i, 0)),
            const((PLE_DIM, d)), const((d, d)), const((1, d)), const((1, d)),
        ],
        out_specs=pl.BlockSpec((step, d), lambda i, *_: (i, 0)),
        scratch_shapes=[pltpu.VMEM((2, COMBINE_PARTS, N_EXPERTS * sw, d), BF16),
                        pltpu.SemaphoreType.DMA((2,)),
                        pltpu.VMEM((sw, d), BF16), pltpu.SemaphoreType.DMA(()),
                        pltpu.VMEM((COMBINE_PARTS, tile, d), F32)],
    )
    return pl.pallas_call(
        functools.partial(_combine_kernel, cap=cap, tile=tile, sw=sw),
        grid_spec=grid_spec,
        out_shape=jax.ShapeDtypeStruct((n, d), F32),
        compiler_params=pltpu.CompilerParams(
            dimension_semantics=("arbitrary",), vmem_limit_bytes=VMEM_LIMIT),
        name="combine",
    )(win_start, win_n, x1, pos_tok, gate_tok, ye.reshape(N_EXPERTS * cap, d), p, w_ple, w_pg, b_pg, fg)


def _trunk(x, p, w):
    b, l, d = x.shape
    n = b * l
    cap = CAPACITY_FACTOR * n // N_EXPERTS
    rows = n // LANES

    x1, tok_rows, probs = _front(x, w["g1"], w["w_in"], w["conv_k"], w["conv_b"], w["ln_g"],
                                 w["ln_b"], w["pool_w"], w["pool_scale"], w["w_out"], w["g2"],
                                 w["w_rt"])

    chunk = min(SLOT_CHUNK, cap)
    n_chunks = cap // chunk
    pos3, rank3, off3, lo3, hi3 = _select(probs.reshape(N_EXPERTS, rows, LANES), cap, chunk)

    lo_row = lo3[:, 0, :n_chunks].reshape(-1)
    hi_row = hi3[:, 0, :n_chunks].reshape(-1)
    off_rows = off3[:, :, 0]
    end_rows = jnp.concatenate([off_rows[:, 1:], jnp.full((N_EXPERTS, 1), cap, I32)], axis=1)
    idx = _invert(rank3, off_rows, end_rows, lo_row, hi_row, n_chunks, chunk)
    ye = _experts(tok_rows, idx, w["w_gate"], w["w_up"], w["w_down"], cap)

    tile = min(COMBINE_TILE, n // COMBINE_PARTS)
    sw = min(SLOT_WINDOW, cap)
    starts = off3[:, ::tile // LANES, 0]
    ends = jnp.concatenate([starts[:, 1:], jnp.full((N_EXPERTS, 1), cap, I32)], axis=1)
    win_start = jnp.minimum(starts // 16 * 16, cap - sw)
    win_n = jnp.maximum((ends - win_start + sw - 1) // sw, 1)
    y = _combine(x1.reshape(n, d), pos3.reshape(N_EXPERTS, n).T, probs.T, ye, p.reshape(n, PLE_DIM),
                 w["w_ple"], w["w_pg"], w["b_pg"], w["fg"],
                 win_start.T.reshape(-1), win_n.T.reshape(-1), cap)
    return y.reshape(b, l, d)


def kernel(x_prompt, x_sample, p_prompt, p_sample, norm1_g, w_in, conv_k, conv_b, conv_ln_g,
           conv_ln_b, pool_w, pool_scale, w_out, norm2_g, w_router, w_gate, w_up, w_down,
           w_ple, w_pg, b_pg, final_g):
    row = lambda v: v.reshape(1, -1).astype(F32)
    w = dict(
        g1=row(norm1_g[0]), w_in=w_in[0].astype(BF16), conv_k=conv_k[0].reshape(KERNEL_SIZE, CONV_W),
        conv_b=row(conv_b[0]), ln_g=row(conv_ln_g[0]), ln_b=row(conv_ln_b[0]),
        pool_w=pool_w[0].astype(BF16), pool_scale=row(pool_scale[0]), w_out=w_out[0].astype(BF16),
        g2=row(norm2_g[0]), w_rt=w_router[0].T.astype(BF16),
        w_gate=w_gate[0].astype(BF16), w_up=w_up[0].astype(BF16), w_down=w_down[0].astype(BF16),
        w_ple=w_ple[0].astype(BF16), w_pg=w_pg[0].astype(BF16), b_pg=row(b_pg[0]), fg=row(final_g),
    )
    return (_trunk(x_prompt, p_prompt[0], w), _trunk(x_sample, p_sample[0], w))
```

```python
import functools

import jax
import jax.numpy as jnp
from jax import lax
from jax.experimental import pallas as pl
from jax.experimental.pallas import tpu as pltpu

D_MODEL = 1024
CONV_W = 512
POOL_W = 512
POOL_GROUPS = 4
POOL_GC = 128
POOL_WINDOWS = (2, 4, 8, 16)
KERNEL_SIZE = 31
IN_COLS = 2 * CONV_W + POOL_W
N_EXPERTS = 16
CAPACITY_FACTOR = 2
D_EXPERT = 2 * D_MODEL
PLE_DIM = 256
EPS = 1e-6

LANES = 128
SUBLANES = 8
HALO = 16
VMEM_LIMIT = 56 * 1024 * 1024
ROW_SUB = D_MODEL // LANES

SEQ_TILE = 512
SLOT_CHUNK = 256
STEP_CHUNKS = 4
INVERT_GROUP = 4
ROW_WINDOW = LANES
COMBINE_TILE = 512
COMBINE_PARTS = 2
SLOT_WINDOW = 128

BF16 = jnp.bfloat16
F32 = jnp.float32
I32 = jnp.int32


def _dot(a, b):
    return jnp.dot(a, b, preferred_element_type=F32)


def _rms(x, g):
    return x * lax.rsqrt(jnp.mean(x * x, axis=-1, keepdims=True) + EPS) * g


def _front_kernel(xp_ref, xm_ref, xn_ref, g1_ref, win_ref, ck_ref, cb_ref, lg_ref, lb_ref,
                  pw_ref, ps_ref, wout_ref, g2_ref, wrt_ref,
                  x1_ref, row_ref, pr_ref, u_ref, zp_ref, *, seq_len, tile):
    t = pl.program_id(1)
    nt = pl.num_programs(1)
    rows = tile + 2 * HALO

    xs = jnp.concatenate([xp_ref[...], xm_ref[...], xn_ref[...]], axis=0)
    h = _rms(xs, g1_ref[...]).astype(BF16)
    z = _dot(h, win_ref[...])

    u_ref[0] = z[:, :CONV_W] * jax.nn.sigmoid(z[:, CONV_W:2 * CONV_W])
    zp_ref[...] = z[:, 2 * CONV_W:]
    for first, keep in ((0, t > 0), (HALO + tile, t < nt - 1)):
        edge = pl.ds(first, HALO)
        scale = jnp.where(keep, 1.0, 0.0).astype(F32)
        u_ref[0, edge, :] = u_ref[0, edge, :] * scale
        zp_ref[edge, :] = zp_ref[edge, :] * scale
    for r in range(1, SUBLANES):
        u_ref[r, pl.ds(0, rows - SUBLANES), :] = u_ref[0, pl.ds(r, rows - SUBLANES), :]

    conv = None
    for k in range(KERNEL_SIZE):
        off = HALO - KERNEL_SIZE // 2 + k
        r = off % SUBLANES
        term = u_ref[r, pl.ds(off - r, tile), :] * ck_ref[k:k + 1, :]
        conv = term if conv is None else conv + term
    conv = conv + cb_ref[...]
    mu = jnp.mean(conv, axis=-1, keepdims=True)
    cen = conv - mu
    var = jnp.mean(cen * cen, axis=-1, keepdims=True)
    yc = cen * lax.rsqrt(var + EPS) * lg_ref[...] + lb_ref[...]
    yc = yc * jax.nn.sigmoid(yc)

    pos = lax.broadcasted_iota(I32, (tile, LANES), 0) + t * tile
    yps = []
    for g, w in enumerate(POOL_WINDOWS):
        cols = pl.ds(g * POOL_GC, POOL_GC)
        s = None
        for j in range(-(w // 2), w // 2):
            term = zp_ref[pl.ds(HALO + j, tile), cols]
            s = term if s is None else s + term
        lo = jnp.maximum(pos - w // 2, 0)
        hi = jnp.minimum(pos + w // 2 - 1, seq_len - 1)
        cnt = (hi - lo + 1).astype(F32)
        pooled = s / cnt - zp_ref[pl.ds(HALO, tile), cols]
        yp = _dot(pooled.astype(BF16), pw_ref[g]) * ps_ref[:, g * POOL_GC:(g + 1) * POOL_GC]
        yps.append(yp)

    y = jnp.concatenate([yc] + yps, axis=1).astype(BF16)
    x1 = xm_ref[...] + _dot(y, wout_ref[...])
    x1_ref[...] = x1

    h2f = _rms(x1, g2_ref[...])
    h2 = h2f.astype(BF16)
    logits = lax.dot_general(wrt_ref[...], h2, (((1,), (1,)), ((), ())),
                             preferred_element_type=F32)
    m = jnp.max(logits, axis=0, keepdims=True)
    ex = jnp.exp(logits - m)
    pr_ref[...] = ex / jnp.sum(ex, axis=0, keepdims=True)
    for s in range(ROW_SUB):
        row_ref[pl.ds(s, tile, stride=ROW_SUB), :] = h2f[:, s * LANES:(s + 1) * LANES]


def _front(x, g1, w_in, conv_k, conv_b, ln_g, ln_b, pool_w, pool_scale, w_out, g2, w_rt):
    b, l, d = x.shape
    tile = min(SEQ_TILE, l)
    nt = l // tile
    hpt = tile // HALO
    nh = l // HALO
    n = b * l
    const = lambda shape: pl.BlockSpec(shape, lambda i, j: (0,) * len(shape))
    kern = functools.partial(_front_kernel, seq_len=l, tile=tile)
    return pl.pallas_call(
        kern,
        grid=(b, nt),
        in_specs=[
            pl.BlockSpec((None, HALO, d), lambda i, j: (i, jnp.maximum(j * hpt - 1, 0), 0)),
            pl.BlockSpec((None, tile, d), lambda i, j: (i, j, 0)),
            pl.BlockSpec((None, HALO, d), lambda i, j: (i, jnp.minimum((j + 1) * hpt, nh - 1), 0)),
            const((1, d)), const((d, IN_COLS)), const((KERNEL_SIZE, CONV_W)), const((1, CONV_W)),
            const((1, CONV_W)), const((1, CONV_W)), const((POOL_GROUPS, POOL_GC, POOL_GC)),
            const((1, POOL_W)), const((d, d)), const((1, d)), const((N_EXPERTS, d)),
        ],
        out_specs=[
            pl.BlockSpec((None, tile, d), lambda i, j: (i, j, 0)),
            pl.BlockSpec((tile * ROW_SUB, LANES), lambda i, j: (i * nt + j, 0)),
            pl.BlockSpec((N_EXPERTS, tile), lambda i, j: (0, i * nt + j)),
        ],
        out_shape=[
            jax.ShapeDtypeStruct((b, l, d), F32),
            jax.ShapeDtypeStruct((n * ROW_SUB, LANES), F32),
            jax.ShapeDtypeStruct((N_EXPERTS, n), F32),
        ],
        scratch_shapes=[pltpu.VMEM((SUBLANES, tile + 2 * HALO, CONV_W), F32),
                        pltpu.VMEM((tile + 2 * HALO, POOL_W), F32)],
        compiler_params=pltpu.CompilerParams(
            dimension_semantics=("arbitrary", "arbitrary"), vmem_limit_bytes=VMEM_LIMIT),
        name="front",
    )(x, x, x, g1, w_in, conv_k, conv_b, ln_g, ln_b, pool_w, pool_scale, w_out, g2, w_rt)


def _select_kernel(p_ref, pos_ref, rank_ref, off_ref, lo_ref, hi_ref, tot_ref, *, cap, chunk):
    e, r, _ = p_ref.shape
    p = p_ref[...]
    capf = jnp.float32(cap)

    def count_ge(thr):
        hit = jnp.where(p >= thr, 1.0, 0.0).astype(F32)
        return jnp.sum(jnp.sum(hit, axis=1, keepdims=True), axis=2, keepdims=True)

    def bisect(i, bits):
        cand = jnp.bitwise_or(bits, jnp.left_shift(jnp.int32(1), 30 - i))
        return jnp.where(count_ge(lax.bitcast_convert_type(cand, F32)) >= capf, cand, bits)

    tau_bits = lax.fori_loop(0, 31, bisect, jnp.zeros((e, 1, 1), I32))
    tau = lax.bitcast_convert_type(tau_bits, F32)
    above_tau = p >= lax.bitcast_convert_type(tau_bits + 1, F32)
    at_tau = jnp.logical_and(p >= tau, jnp.logical_not(above_tau))

    gt = jnp.where(above_tau, 1.0, 0.0).astype(BF16).reshape(e * r, LANES)
    eq = jnp.where(at_tau, 1.0, 0.0).astype(BF16).reshape(e * r, LANES)
    ri = lax.broadcasted_iota(I32, (LANES, LANES), 0)
    ci = lax.broadcasted_iota(I32, (LANES, LANES), 1)
    upper = jnp.where(ri <= ci, 1.0, 0.0).astype(BF16)
    ones = jnp.ones((LANES, LANES), BF16)
    rr = lax.broadcasted_iota(I32, (r, r), 0)
    rc = lax.broadcasted_iota(I32, (r, r), 1)
    lower = jnp.where(rc < rr, 1.0, 0.0).astype(BF16)

    def prefix(mask):
        within = _dot(mask, upper).reshape(e, r, LANES)
        tot_ref[...] = _dot(mask, ones).reshape(e, r, LANES)
        offs = [_dot(lower, tot_ref[i].astype(BF16)) for i in range(e)]
        return within, jnp.stack(offs, axis=0), tot_ref[...]

    gt_in, gt_off, gt_tot = prefix(gt)
    eq_in, eq_off, eq_tot = prefix(eq)
    n_gt = gt_off[:, r - 1:r, :] + gt_tot[:, r - 1:r, :]
    need = capf - n_gt
    eq_incl = eq_in + eq_off
    sel = jnp.logical_or(above_tau, jnp.logical_and(at_tau, eq_incl <= need))
    rank = gt_in + gt_off + jnp.minimum(eq_incl, need)
    rank_ref[...] = rank.astype(I32)
    pos_ref[...] = jnp.where(sel, rank - 1.0, -1.0).astype(I32)
    off_ref[...] = (gt_off + jnp.minimum(eq_off, need)).astype(I32)
    row_end = gt_off + gt_tot + jnp.minimum(eq_off + eq_tot, need)
    first = (lax.broadcasted_iota(I32, (1, 1, LANES), 2) * chunk).astype(F32)
    lo_ref[...] = jnp.sum(jnp.where(row_end <= first, 1.0, 0.0), axis=1, keepdims=True).astype(I32)
    hi_ref[...] = jnp.sum(jnp.where(row_end <= first + (chunk - 1.0), 1.0, 0.0),
                          axis=1, keepdims=True).astype(I32)


def _select(probs3, cap, chunk):
    e, r, _ = probs3.shape
    full = lambda s: pl.BlockSpec(s, lambda: (0,) * len(s))
    big = jax.ShapeDtypeStruct((e, r, LANES), I32)
    small = jax.ShapeDtypeStruct((e, 1, LANES), I32)
    return pl.pallas_call(
        functools.partial(_select_kernel, cap=cap, chunk=chunk),
        in_specs=[full((e, r, LANES))],
        out_specs=[full((e, r, LANES))] * 3 + [full((e, 1, LANES))] * 2,
        out_shape=[big, big, big, small, small],
        scratch_shapes=[pltpu.VMEM((e, r, LANES), F32)],
        compiler_params=pltpu.CompilerParams(vmem_limit_bytes=VMEM_LIMIT),
        name="select",
    )(probs3)


def _invert_kernel(lo_ref, hi_ref, rank_ref, off_ref, end_ref, idx_ref, acc_ref,
                   *, n_chunks, chunk, win):
    e = pl.program_id(0)
    local_row = lax.broadcasted_iota(I32, (win, LANES), 0).astype(F32).astype(BF16)
    ones = jnp.ones((win, LANES), BF16)
    srel = lax.broadcasted_iota(I32, (chunk, LANES), 0).astype(F32)

    def window(c, w):
        first = c * chunk
        row0 = pl.multiple_of(w * win, win)
        slot = lax.broadcasted_iota(I32, (chunk, win), 0) + first
        hit = jnp.logical_and(off_ref[:, pl.ds(row0, win)] <= slot, slot < end_ref[:, pl.ds(row0, win)])
        rel = jnp.clip(rank_ref[pl.ds(row0, win), :] - first, -1, chunk).astype(F32).astype(BF16)
        got = _dot(jnp.where(hit, 1.0, 0.0).astype(BF16),
                   jnp.concatenate([rel, local_row, ones], axis=1))
        cnt = jnp.sum(jnp.where(got[:, :LANES] <= srel, 1.0, 0.0), axis=1, keepdims=True)
        row = got[:, LANES:2 * LANES] + (w * win).astype(F32)
        return got[:, 2 * LANES:] * (row * LANES + cnt)

    def group(g, carry):
        base = g * INVERT_GROUP
        for j in range(INVERT_GROUP):
            c = base + j
            acc_ref[j] = window(c, lo_ref[e * n_chunks + c] // win)
        for j in range(INVERT_GROUP):
            c = base + j

            def more(w, carry2, j=j, c=c):
                acc_ref[j] += window(c, w)
                return carry2

            lax.fori_loop(lo_ref[e * n_chunks + c] // win + 1, hi_ref[e * n_chunks + c] // win + 1, more, 0)
        for j in range(INVERT_GROUP):
            idx_ref[pl.ds(base + j, 1), :] = acc_ref[j].T[0:1, :].astype(I32)
        return carry

    lax.fori_loop(0, n_chunks // INVERT_GROUP, group, 0)


def _invert(rank3, off_rows, end_rows, lo_row, hi_row, n_chunks, chunk):
    e, r, _ = rank3.shape
    win = min(ROW_WINDOW, r)
    grid_spec = pltpu.PrefetchScalarGridSpec(
        num_scalar_prefetch=2,
        grid=(e,),
        in_specs=[pl.BlockSpec((None, r, LANES), lambda i, *_: (i, 0, 0)),
                  pl.BlockSpec((None, 1, r), lambda i, *_: (i, 0, 0)),
                  pl.BlockSpec((None, 1, r), lambda i, *_: (i, 0, 0))],
        out_specs=pl.BlockSpec((None, n_chunks, chunk), lambda i, *_: (i, 0, 0)),
        scratch_shapes=[pltpu.VMEM((INVERT_GROUP, chunk, LANES), F32)],
    )
    return pl.pallas_call(
        functools.partial(_invert_kernel, n_chunks=n_chunks, chunk=chunk, win=win),
        grid_spec=grid_spec,
        out_shape=jax.ShapeDtypeStruct((e, n_chunks, chunk), I32),
        compiler_params=pltpu.CompilerParams(dimension_semantics=("arbitrary",)),
        name="invert",
    )(lo_row, hi_row, rank3, off_rows.reshape(e, 1, r), end_rows.reshape(e, 1, r))


def _experts_kernel(cur_ref, nxt_ref, row_hbm, wg_ref, wu_ref, wd_ref, ye_ref, buf, sem,
                    xe_ref, *, chunk):
    step = pl.program_id(0) * pl.num_programs(1) + pl.program_id(1)
    n_steps = pl.num_programs(0) * pl.num_programs(1)
    slot = lax.rem(step, 2)

    def gather(idx_ref, s, h):
        for r in range(chunk):
            src = pl.multiple_of(idx_ref[0, h * chunk + r] * ROW_SUB, ROW_SUB)
            pltpu.make_async_copy(row_hbm.at[pl.ds(src, ROW_SUB), :],
                                  buf.at[s, h, pl.ds(r * ROW_SUB, ROW_SUB), :], sem.at[s, h]).start()

    def wait(s, h):
        pltpu.make_async_copy(row_hbm.at[pl.ds(0, chunk * ROW_SUB), :], buf.at[s, h], sem.at[s, h]).wait()

    def stage(s, h):
        for k in range(ROW_SUB):
            xe_ref[h, :, k * LANES:(k + 1) * LANES] = (
                buf[s, h, pl.ds(k, chunk, stride=ROW_SUB), :].astype(BF16))

    @pl.when(step == 0)
    def _():
        for h in range(STEP_CHUNKS):
            gather(cur_ref, 0, h)
        wait(0, 0)
        stage(0, 0)

    for h in range(1, STEP_CHUNKS):
        wait(slot, h)
    for h in range(STEP_CHUNKS):
        gather(nxt_ref, 1 - slot, h)
        if h > 0:
            stage(slot, h)
        xe = xe_ref[h]
        a = _dot(xe, wg_ref[...])
        b = _dot(xe, wu_ref[...])
        hid = (a * jax.nn.sigmoid(a) * b).astype(BF16)
        ye_ref[h * chunk:(h + 1) * chunk, :] = _dot(hid, wd_ref[...]).astype(BF16)
    wait(1 - slot, 0)
    stage(1 - slot, 0)

    @pl.when(step == n_steps - 1)
    def _():
        for h in range(1, STEP_CHUNKS):
            wait(1 - slot, h)


def _experts(rows, idx, wg, wu, wd, cap):
    d = D_MODEL
    e, n_chunks, chunk = idx.shape
    per_step = STEP_CHUNKS * chunk
    n_inner = n_chunks // STEP_CHUNKS
    n_steps = e * n_inner
    idx = idx.reshape(n_steps, 1, per_step)
    smem = lambda f: pl.BlockSpec((None, 1, per_step), f, memory_space=pltpu.SMEM)
    return pl.pallas_call(
        functools.partial(_experts_kernel, chunk=chunk),
        grid=(e, n_inner),
        in_specs=[
            smem(lambda i, c: (i * n_inner + c, 0, 0)),
            smem(lambda i, c: (jnp.minimum(i * n_inner + c + 1, n_steps - 1), 0, 0)),
            pl.BlockSpec(memory_space=pl.ANY),
            pl.BlockSpec((None, d, D_EXPERT), lambda i, c: (i, 0, 0)),
            pl.BlockSpec((None, d, D_EXPERT), lambda i, c: (i, 0, 0)),
            pl.BlockSpec((None, D_EXPERT, d), lambda i, c: (i, 0, 0)),
        ],
        out_specs=pl.BlockSpec((None, per_step, d), lambda i, c: (i, c, 0)),
        out_shape=jax.ShapeDtypeStruct((e, cap, d), BF16),
        scratch_shapes=[pltpu.VMEM((2, STEP_CHUNKS, chunk * ROW_SUB, LANES), F32),
                        pltpu.SemaphoreType.DMA((2, STEP_CHUNKS)),
                        pltpu.VMEM((STEP_CHUNKS, chunk, D_MODEL), BF16)],
        compiler_params=pltpu.CompilerParams(
            dimension_semantics=("arbitrary", "arbitrary"), vmem_limit_bytes=VMEM_LIMIT),
        name="experts",
    )(idx, idx, rows, wg, wu, wd)


def _combine_kernel(ws_ref, wn_ref, x1_ref, pos_ref, pr_ref, ye_hbm, p_ref, wple_ref, wpg_ref, bpg_ref,
                    fg_ref, y_ref, ybuf, sem, xbuf, xsem, acc_ref, *, cap, tile, sw):
    i = pl.program_id(0)
    n = pl.num_programs(0)
    slot = lax.rem(i, 2)

    def copies(step, s):
        return [pltpu.make_async_copy(
            ye_hbm.at[pl.ds(pl.multiple_of(
                e * cap + ws_ref[(step * COMBINE_PARTS + q) * N_EXPERTS + e], 16), sw), :],
            ybuf.at[s, q, pl.ds(e * sw, sw), :], sem.at[s])
            for q in range(COMBINE_PARTS) for e in range(N_EXPERTS)]

    @pl.when(i == 0)
    def _():
        for cp in copies(0, 0):
            cp.start()

    @pl.when(i + 1 < n)
    def _():
        for cp in copies(i + 1, 1 - slot):
            cp.start()

    for cp in copies(i, slot):
        cp.wait()

    lane = lax.broadcasted_iota(I32, (tile, sw), 1)
    for q in range(COMBINE_PARTS):
        part = (i * COMBINE_PARTS + q) * N_EXPERTS
        pos = pos_ref[q * tile:(q + 1) * tile, :]
        gates = pr_ref[q * tile:(q + 1) * tile, :]
        hits = []
        for e in range(N_EXPERTS):
            rel = pos[:, e:e + 1] - ws_ref[part + e]
            hits.append(jnp.where(rel == lane, gates[:, e:e + 1], 0.0).astype(BF16))
        acc_ref[q] = _dot(jnp.concatenate(hits, axis=1), ybuf[slot, q])

    for q in range(COMBINE_PARTS):
        part = (i * COMBINE_PARTS + q) * N_EXPERTS
        for e in range(N_EXPERTS):
            first = ws_ref[part + e]

            def extra(w, carry, q=q, e=e, first=first):
                want = first + w * sw
                start = pl.multiple_of(jnp.minimum(want, cap - sw), 16)
                cp = pltpu.make_async_copy(ye_hbm.at[pl.ds(e * cap + start, sw), :], xbuf, xsem)
                cp.start()
                cp.wait()
                col = pos_ref[q * tile:(q + 1) * tile, e:e + 1]
                hit = jnp.logical_and(col - start == lane, col >= want)
                weight = jnp.where(hit, pr_ref[q * tile:(q + 1) * tile, e:e + 1], 0.0)
                acc_ref[q] += _dot(weight.astype(BF16), xbuf[...])
                return carry

            lax.fori_loop(1, wn_ref[part + e], extra, 0)

    for q in range(COMBINE_PARTS):
        rows = slice(q * tile, (q + 1) * tile)
        x2 = x1_ref[rows, :] + acc_ref[q]
        gate = jax.nn.sigmoid(_dot(x2.astype(BF16), wpg_ref[...]) + bpg_ref[...])
        x3 = x2 + gate * _dot(p_ref[rows, :].astype(BF16), wple_ref[...])
        y_ref[rows, :] = _rms(x3, fg_ref[...])


def _combine(x1, pos_tok, gate_tok, ye, p, w_ple, w_pg, b_pg, fg, win_start, win_n, cap):
    n, d = x1.shape
    tile = min(COMBINE_TILE, n // COMBINE_PARTS)
    step = tile * COMBINE_PARTS
    sw = min(SLOT_WINDOW, cap)
    const = lambda shape: pl.BlockSpec(shape, lambda i, *_: (0,) * len(shape))
    grid_spec = pltpu.PrefetchScalarGridSpec(
        num_scalar_prefetch=2,
        grid=(n // step,),
        in_specs=[
            pl.BlockSpec((step, d), lambda i, *_: (i, 0)),
            pl.BlockSpec((step, N_EXPERTS), lambda i, *_: (i, 0)),
            pl.BlockSpec((step, N_EXPERTS), lambda i, *_: (i, 0)),
            pl.BlockSpec(memory_space=pl.ANY),
            pl.BlockSpec((step, PLE_DIM), lambda i, *_: (i, 0)),
            const((PLE_DIM, d)), const((d, d)), const((1, d)), const((1, d)),
        ],
        out_specs=pl.BlockSpec((step, d), lambda i, *_: (i, 0)),
        scratch_shapes=[pltpu.VMEM((2, COMBINE_PARTS, N_EXPERTS * sw, d), BF16),
                        pltpu.SemaphoreType.DMA((2,)),
                        pltpu.VMEM((sw, d), BF16), pltpu.SemaphoreType.DMA(()),
                        pltpu.VMEM((COMBINE_PARTS, tile, d), F32)],
    )
    return pl.pallas_call(
        functools.partial(_combine_kernel, cap=cap, tile=tile, sw=sw),
        grid_spec=grid_spec,
        out_shape=jax.ShapeDtypeStruct((n, d), F32),
        compiler_params=pltpu.CompilerParams(
            dimension_semantics=("arbitrary",), vmem_limit_bytes=VMEM_LIMIT),
        name="combine",
    )(win_start, win_n, x1, pos_tok, gate_tok, ye.reshape(N_EXPERTS * cap, d), p, w_ple, w_pg, b_pg, fg)


def _trunk(x, p, w):
    b, l, d = x.shape
    n = b * l
    cap = CAPACITY_FACTOR * n // N_EXPERTS
    rows = n // LANES

    x1, tok_rows, probs = _front(x, w["g1"], w["w_in"], w["conv_k"], w["conv_b"], w["ln_g"],
                                 w["ln_b"], w["pool_w"], w["pool_scale"], w["w_out"], w["g2"],
                                 w["w_rt"])

    chunk = min(SLOT_CHUNK, cap)
    n_chunks = cap // chunk
    pos3, rank3, off3, lo3, hi3 = _select(probs.reshape(N_EXPERTS, rows, LANES), cap, chunk)

    lo_row = lo3[:, 0, :n_chunks].reshape(-1)
    hi_row = hi3[:, 0, :n_chunks].reshape(-1)
    off_rows = off3[:, :, 0]
    end_rows = jnp.concatenate([off_rows[:, 1:], jnp.full((N_EXPERTS, 1), cap, I32)], axis=1)
    idx = _invert(rank3, off_rows, end_rows, lo_row, hi_row, n_chunks, chunk)
    ye = _experts(tok_rows, idx, w["w_gate"], w["w_up"], w["w_down"], cap)

    tile = min(COMBINE_TILE, n // COMBINE_PARTS)
    sw = min(SLOT_WINDOW, cap)
    starts = off3[:, ::tile // LANES, 0]
    ends = jnp.concatenate([starts[:, 1:], jnp.full((N_EXPERTS, 1), cap, I32)], axis=1)
    win_start = jnp.minimum(starts // 16 * 16, cap - sw)
    win_n = jnp.maximum((ends - win_start + sw - 1) // sw, 1)
    y = _combine(x1.reshape(n, d), pos3.reshape(N_EXPERTS, n).T, probs.T, ye, p.reshape(n, PLE_DIM),
                 w["w_ple"], w["w_pg"], w["b_pg"], w["fg"],
                 win_start.T.reshape(-1), win_n.T.reshape(-1), cap)
    return y.reshape(b, l, d)


def kernel(x_prompt, x_sample, p_prompt, p_sample, norm1_g, w_in, conv_k, conv_b, conv_ln_g,
           conv_ln_b, pool_w, pool_scale, w_out, norm2_g, w_router, w_gate, w_up, w_down,
           w_ple, w_pg, b_pg, final_g):
    row = lambda v: v.reshape(1, -1).astype(F32)
    w = dict(
        g1=row(norm1_g[0]), w_in=w_in[0].astype(BF16), conv_k=conv_k[0].reshape(KERNEL_SIZE, CONV_W),
        conv_b=row(conv_b[0]), ln_g=row(conv_ln_g[0]), ln_b=row(conv_ln_b[0]),
        pool_w=pool_w[0].astype(BF16), pool_scale=row(pool_scale[0]), w_out=w_out[0].astype(BF16),
        g2=row(norm2_g[0]), w_rt=w_router[0].T.astype(BF16),
        w_gate=w_gate[0].astype(BF16), w_up=w_up[0].astype(BF16), w_down=w_down[0].astype(BF16),
        w_ple=w_ple[0].astype(BF16), w_pg=w_pg[0].astype(BF16), b_pg=row(b_pg[0]), fg=row(final_g),
    )
    return (_trunk(x_prompt, p_prompt[0], w), _trunk(x_sample, p_sample[0], w))
```
